```python
import math
import jax, jax.numpy as jnp
from jax import lax
import numpy as np

D_MODEL = 2048
BATCH = 4
SEQ = 4096
DEPTH = 4

N_A = DEPTH // 2
N_B = DEPTH - N_A
SSM_WIDTH = D_MODEL
SSM_GROUP = 16
SSM_GROUPS = SSM_WIDTH // SSM_GROUP
SSM_STATE = 64
N_HEADS = 16
N_KV = 4
HPG = N_HEADS // N_KV
HEAD_DIM = D_MODEL // N_HEADS
ATT_WIDTH = N_HEADS * HEAD_DIM
N_BRANCH = 3
CMP_LEN = 32
CMP_STRIDE = 16
SEL_LEN = 64
SEL_TOPK = 16
WINDOW = 512
WIN_QBLOCK = 128
SEL_QCHUNK = 32
SEL_BONUS = 1e3
NEG = -1e30
EPS = 1e-6

kernel_name = "yoco_s5_nsa_hybrid"


def rmsnorm(x, g):
    xf = x.astype(jnp.float32)
    y = xf * lax.rsqrt(jnp.mean(xf * xf, axis=-1, keepdims=True) + EPS)
    return (y * g.astype(jnp.float32)).astype(x.dtype)


def modulate(h, shift, scale):
    return h * (1.0 + scale[:, None, :]) + shift[:, None, :]


def masked_softmax(s, mask):
    s = jnp.where(mask, s.astype(jnp.float32), NEG)
    p = jax.nn.softmax(s, axis=-1)
    return jnp.where(mask, p, 0.0)


def s5_discretize(lam_re, lam_im, log_step, b_re, b_im):
    dt = jnp.exp(log_step.astype(jnp.float32))[:, None]
    lr, li = lam_re.astype(jnp.float32), lam_im.astype(jnp.float32)
    mag = jnp.exp(lr * dt)
    a_re, a_im = mag * jnp.cos(li * dt), mag * jnp.sin(li * dt)
    den = lr * lr + li * li
    coef_re = ((a_re - 1.0) * lr + a_im * li) / den
    coef_im = (a_im * lr - (a_re - 1.0) * li) / den
    br, bi = b_re.astype(jnp.float32), b_im.astype(jnp.float32)
    bb_re = coef_re[..., None] * br - coef_im[..., None] * bi
    bb_im = coef_re[..., None] * bi + coef_im[..., None] * br
    return a_re, a_im, bb_re, bb_im


def _ssm_combine(left, right):
    ar_i, ai_i, br_i, bi_i = left
    ar_j, ai_j, br_j, bi_j = right
    return (ar_j * ar_i - ai_j * ai_i,
            ar_j * ai_i + ai_j * ar_i,
            ar_j * br_i - ai_j * bi_i + br_j,
            ar_j * bi_i + ai_j * br_i + bi_j)


def _scan_one(a_re, a_im, bu_re, bu_im):
    out = lax.associative_scan(_ssm_combine, (a_re, a_im, bu_re, bu_im), axis=0)
    return out[2], out[3]


def s5_mixer(h, w_in, lam_re, lam_im, log_step, b_re, b_im, c_re, c_im, d_skip, w_glu, b_glu, w_out):
    B_, L, _ = h.shape
    u, z = jnp.split(h @ w_in, 2, axis=-1)
    ug = u.reshape(B_, L, SSM_GROUPS, SSM_GROUP).astype(jnp.float32)
    a_re, a_im, bb_re, bb_im = s5_discretize(lam_re, lam_im, log_step, b_re, b_im)
    bu_re = jnp.einsum('blgc,gnc->blgn', ug, bb_re)
    bu_im = jnp.einsum('blgc,gnc->blgn', ug, bb_im)
    shp = (L, SSM_GROUPS, SSM_STATE)
    x_re, x_im = jax.vmap(_scan_one, in_axes=(None, None, 0, 0))(
        jnp.broadcast_to(a_re, shp), jnp.broadcast_to(a_im, shp), bu_re, bu_im)
    y = (jnp.einsum('blgn,gcn->blgc', x_re, c_re.astype(jnp.float32))
         - jnp.einsum('blgn,gcn->blgc', x_im, c_im.astype(jnp.float32)))
    y = y + d_skip.astype(jnp.float32).reshape(SSM_GROUPS, SSM_GROUP) * ug
    y = jax.nn.gelu(y.reshape(B_, L, SSM_WIDTH).astype(h.dtype))
    y = y * jax.nn.sigmoid(y @ w_glu + b_glu)
    return (y * jax.nn.silu(z)) @ w_out


def compress_blocks(t, blk_idx, pe, w1, b1, w2, b2):
    blocks = t[:, blk_idx] + pe[None, None, :, None, :]
    B_, n = blocks.shape[:2]
    flat = jnp.moveaxis(blocks, 3, 2).reshape(B_, n, N_KV, CMP_LEN * HEAD_DIM)
    return jax.nn.gelu(flat @ w1 + b1) @ w2 + b2


def nsa_shared_kv(h_kv, w_kv, cmp_pe, cmp_w1, cmp_b1, cmp_w2, cmp_b2):
    B_, L, _ = h_kv.shape
    kv = (h_kv @ w_kv).reshape(B_, L, 2 * N_BRANCH, N_KV, HEAD_DIM)
    n_cmp = (L - CMP_LEN) // CMP_STRIDE + 1
    blk_idx = np.arange(n_cmp)[:, None] * CMP_STRIDE + np.arange(CMP_LEN)[None, :]
    kc = compress_blocks(kv[:, :, 0], blk_idx, cmp_pe[0], cmp_w1[0], cmp_b1[0], cmp_w2[0], cmp_b2[0])
    vc = compress_blocks(kv[:, :, 1], blk_idx, cmp_pe[1], cmp_w1[1], cmp_b1[1], cmp_w2[1], cmp_b2[1])
    return kc, vc, kv[:, :, 2], kv[:, :, 3], kv[:, :, 4], kv[:, :, 5]


def cmp_attention(q, kc, vc, pos):
    n_cmp = kc.shape[1]
    s = jnp.einsum('blghd,bngd->bghln', q, kc)
    blk_end = jnp.arange(n_cmp) * CMP_STRIDE + CMP_LEN - 1
    mask = blk_end[None, :] <= pos[:, None]
    p = masked_softmax(s, mask)
    o = jnp.einsum('bghln,bngd->blghd', p.astype(vc.dtype), vc)
    return o, p.sum(axis=2)


def selection_overlap(n_cmp, n_slc):
    c0 = np.arange(n_cmp)[:, None] * CMP_STRIDE
    s0 = np.arange(n_slc)[None, :] * SEL_LEN
    ov = np.clip(np.minimum(c0 + CMP_LEN, s0 + SEL_LEN) - np.maximum(c0, s0), 0, None)
    return (ov / CMP_STRIDE).astype(np.float32)


def select_blocks(p_cmp, pos):
    n_cmp = p_cmp.shape[-1]
    n_slc = pos.shape[0] // SEL_LEN
    p_slc = jnp.einsum('bgln,ns->bgls', p_cmp, jnp.asarray(selection_overlap(n_cmp, n_slc)))
    blk = jnp.arange(n_slc)[None, :]
    cur = (pos // SEL_LEN)[:, None]
    valid = blk <= cur
    forced = ((blk == 0) | (blk == cur) | (blk == cur - 1)).astype(jnp.float32)
    score = jnp.where(valid, p_slc + SEL_BONUS * forced, -SEL_BONUS)
    top, idx = lax.top_k(score, min(SEL_TOPK, n_slc))
    return idx, top > -0.5 * SEL_BONUS


def selected_attention(q, ks, vs, idx, ok):
    B_, L = q.shape[:2]
    n_slc = L // SEL_LEN
    n_ch = L // SEL_QCHUNK
    kb = jnp.moveaxis(ks.reshape(B_, n_slc, SEL_LEN, N_KV, HEAD_DIM), 3, 1)
    vb = jnp.moveaxis(vs.reshape(B_, n_slc, SEL_LEN, N_KV, HEAD_DIM), 3, 1)
    q_ch = jnp.moveaxis(q.reshape(B_, n_ch, SEL_QCHUNK, N_KV, HPG, HEAD_DIM), 1, 0)
    idx_ch = jnp.moveaxis(idx.reshape(B_, N_KV, n_ch, SEL_QCHUNK, -1), 2, 0)
    ok_ch = jnp.moveaxis(ok.reshape(B_, N_KV, n_ch, SEL_QCHUNK, -1), 2, 0)
    gather = jax.vmap(jax.vmap(lambda blocks, ix: blocks[ix]))
    offs = jnp.arange(SEL_LEN)

    def chunk(args):
        ci, qc, ic, oc = args
        t = ci * SEL_QCHUNK + jnp.arange(SEL_QCHUNK)
        kg = gather(kb, ic)
        vg = gather(vb, ic)
        s = jnp.einsum('bqghd,bgqksd->bgqhks', qc, kg)
        kpos = ic[..., None] * SEL_LEN + offs
        mask = (oc[..., None] & (kpos <= t[None, None, :, None, None]))[:, :, :, None]
        p = masked_softmax(s.reshape(*s.shape[:4], -1), mask.reshape(*mask.shape[:4], -1))
        p = p.reshape(s.shape).astype(vg.dtype)
        return jnp.einsum('bgqhks,bgqksd->bqghd', p, vg)

    o = lax.map(chunk, (jnp.arange(n_ch), q_ch, idx_ch, ok_ch))
    return jnp.moveaxis(o, 0, 1).reshape(B_, L, N_KV, HPG, HEAD_DIM)


def window_attention(q, kw, vw):
    B_, L = q.shape[:2]
    nb = L // WIN_QBLOCK
    span = WIN_QBLOCK + WINDOW
    pad = ((0, 0), (WINDOW, 0), (0, 0), (0, 0))
    kp, vp = jnp.pad(kw, pad), jnp.pad(vw, pad)
    q_blk = jnp.moveaxis(q.reshape(B_, nb, WIN_QBLOCK, N_KV, HPG, HEAD_DIM), 1, 0)
    koff = np.arange(span) - WINDOW
    rel = np.arange(WIN_QBLOCK)[:, None] - koff[None, :]
    band = (rel >= 0) & (rel < WINDOW)

    def block(args):
        bi, qb = args
        start = bi * WIN_QBLOCK
        kb = lax.dynamic_slice_in_dim(kp, start, span, axis=1)
        vb = lax.dynamic_slice_in_dim(vp, start, span, axis=1)
        mask = band & ((start + koff) >= 0)[None, :]
        s = jnp.einsum('bqghd,bkgd->bghqk', qb, kb)
        p = masked_softmax(s, mask).astype(vb.dtype)
        return jnp.einsum('bghqk,bkgd->bqghd', p, vb)

    o = lax.map(block, (jnp.arange(nb), q_blk))
    return jnp.moveaxis(o, 0, 1).reshape(B_, L, N_KV, HPG, HEAD_DIM)


def nsa_mixer(h, w_qg, w_o, kc, vc, ks, vs, kw, vw):
    B_, L, _ = h.shape
    proj = h @ w_qg
    q = proj[..., :ATT_WIDTH].reshape(B_, L, N_KV, HPG, HEAD_DIM) * (HEAD_DIM ** -0.5)
    g_end = ATT_WIDTH + N_BRANCH * N_HEADS
    gates = jax.nn.sigmoid(proj[..., ATT_WIDTH:g_end].astype(jnp.float32)).astype(h.dtype)
    gates = gates.reshape(B_, L, N_BRANCH, N_KV, HPG, 1)
    z = proj[..., g_end:].reshape(B_, L, N_BRANCH, N_KV, HPG, HEAD_DIM)
    pos = jnp.arange(L)
    o_cmp, p_cmp = cmp_attention(q, kc, vc, pos)
    idx, ok = select_blocks(p_cmp, pos)
    o_sel = selected_attention(q, ks, vs, idx, ok)
    o_win = window_attention(q, kw, vw)
    o_br = jnp.stack([o_cmp, o_sel, o_win], axis=2)
    o = jnp.sum(gates * jax.nn.silu(z) * o_br, axis=2)
    return o.reshape(B_, L, ATT_WIDTH) @ w_o


def setup_inputs(seed: int = 0) -> dict:
    key = jax.random.key(seed)
    keys = list(jax.random.split(key, 40))

    def nrm(shape, s):
        return s * jax.random.normal(keys.pop(), shape, jnp.float32)

    D, E, G, N = D_MODEL, SSM_WIDTH, SSM_GROUPS, SSM_STATE
    qg_cols = ATT_WIDTH + N_BRANCH * N_HEADS + N_BRANCH * ATT_WIDTH
    lam_im0 = jnp.pi * jnp.arange(N, dtype=jnp.float32)
    return {
        "x": nrm((BATCH, SEQ, D), 1.0),
        "c": nrm((BATCH, D), 1.0),
        "norm_g": 1.0 + nrm((DEPTH, D), 0.02),
        "mod_w": nrm((DEPTH, D, 3 * D), 0.5 * D ** -0.5),
        "mod_b": nrm((DEPTH, 3 * D), 0.01),
        "ssm_w_in": nrm((N_A, D, 2 * E), D ** -0.5),
        "ssm_lam_re": -0.5 + nrm((N_A, G, N), 0.01),
        "ssm_lam_im": lam_im0 + nrm((N_A, G, N), 0.01),
        "ssm_log_step": jax.random.uniform(keys.pop(), (N_A, G), jnp.float32, math.log(1e-3), math.log(1e-1)),
        "ssm_b_re": nrm((N_A, G, N, SSM_GROUP), (2 * SSM_GROUP) ** -0.5),
        "ssm_b_im": nrm((N_A, G, N, SSM_GROUP), (2 * SSM_GROUP) ** -0.5),
        "ssm_c_re": nrm((N_A, G, SSM_GROUP, N), 0.5),
        "ssm_c_im": nrm((N_A, G, SSM_GROUP, N), 0.5),
        "ssm_d": nrm((N_A, E), 1.0),
        "ssm_w_glu": nrm((N_A, E, E), E ** -0.5),
        "ssm_b_glu": nrm((N_A, E), 0.01),
        "ssm_w_out": nrm((N_A, E, D), E ** -0.5),
        "kv_norm_g": 1.0 + nrm((D,), 0.02),
        "kv_mod_w": nrm((D, 2 * D), 0.5 * D ** -0.5),
        "kv_mod_b": nrm((2 * D,), 0.01),
        "w_kv": nrm((D, 2 * N_BRANCH * N_KV * HEAD_DIM), D ** -0.5),
        "cmp_pe": nrm((2, CMP_LEN, HEAD_DIM), 0.02),
        "cmp_w1": nrm((2, CMP_LEN * HEAD_DIM, HEAD_DIM), (CMP_LEN * HEAD_DIM) ** -0.5),
        "cmp_b1": nrm((2, HEAD_DIM), 0.01),
        "cmp_w2": nrm((2, HEAD_DIM, HEAD_DIM), HEAD_DIM ** -0.5),
        "cmp_b2": nrm((2, HEAD_DIM), 0.01),
        "nsa_w_qg": nrm((N_B, D, qg_cols), D ** -0.5),
        "nsa_w_o": nrm((N_B, ATT_WIDTH, D), ATT_WIDTH ** -0.5),
        "final_norm_g": 1.0 + nrm((D,), 0.02),
    }


def reference(x, c, norm_g, mod_w, mod_b, ssm_w_in, ssm_lam_re, ssm_lam_im, ssm_log_step,
              ssm_b_re, ssm_b_im, ssm_c_re, ssm_c_im, ssm_d, ssm_w_glu, ssm_b_glu, ssm_w_out,
              kv_norm_g, kv_mod_w, kv_mod_b, w_kv, cmp_pe, cmp_w1, cmp_b1, cmp_w2, cmp_b2,
              nsa_w_qg, nsa_w_o, final_norm_g):
    c_act = jax.nn.silu(c)
    shared = None
    for layer in range(DEPTH):
        if layer == N_A:
            kv_shift, kv_scale = jnp.split(c_act @ kv_mod_w + kv_mod_b, 2, axis=-1)
            h_kv = modulate(rmsnorm(x, kv_norm_g), kv_shift, kv_scale)
            shared = nsa_shared_kv(h_kv, w_kv, cmp_pe, cmp_w1, cmp_b1, cmp_w2, cmp_b2)
        shift, scale, gate = jnp.split(c_act @ mod_w[layer] + mod_b[layer], 3, axis=-1)
        h = modulate(rmsnorm(x, norm_g[layer]), shift, scale)
        if layer < N_A:
            out = s5_mixer(h, ssm_w_in[layer], ssm_lam_re[layer], ssm_lam_im[layer],
                           ssm_log_step[layer], ssm_b_re[layer], ssm_b_im[layer],
                           ssm_c_re[layer], ssm_c_im[layer], ssm_d[layer],
                           ssm_w_glu[layer], ssm_b_glu[layer], ssm_w_out[layer])
        else:
            j = layer - N_A
            out = nsa_mixer(h, nsa_w_qg[j], nsa_w_o[j], *shared)
        x = x + gate[:, None, :] * out
    return rmsnorm(x, final_norm_g)
```

```python
import functools
import math

import jax
import jax.numpy as jnp
import numpy as np
from jax import lax
from jax.experimental import pallas as pl
from jax.experimental.pallas import tpu as pltpu

F32 = jnp.float32
BF16 = jnp.bfloat16

SSM_GROUP = 16
SSM_STATE = 64
N_HEADS = 16
N_KV = 4
HPG = N_HEADS // N_KV
HEAD_DIM = 128
N_BRANCH = 3
CMP_LEN = 32
CMP_STRIDE = 16
SEL_LEN = 64
SEL_TOPK = 16
WINDOW = 512
SEL_BONUS = 1e3
NEG = -1e30
EPS = 1e-6

LANES = 128
SUBLANES = 8
MXU_DIM = 256
VMEM_LIMIT = 56 * 1024 * 1024

GROUPS_PER_BLOCK = MXU_DIM // SSM_GROUP
STATES_PER_BLOCK = GROUPS_PER_BLOCK * SSM_STATE


def _cparams(sem):
    return pltpu.CompilerParams(dimension_semantics=sem, vmem_limit_bytes=VMEM_LIMIT)


def _gelu_tanh(x):
    return x * (0.5 * (1.0 + jnp.tanh(math.sqrt(2.0 / math.pi) * (x + 0.044715 * (x * x * x)))))


def _sigmoid(x):
    return 1.0 / (1.0 + jnp.exp(-x))


def _dot_nt(a, b):
    return lax.dot_general(a, b, (((1,), (1,)), ((), ())), preferred_element_type=F32)


def _cond_kernel(c_ref, w_ref, b_ref, o_ref):
    c = c_ref[...]
    ca = (c * _sigmoid(c)).astype(BF16)
    acc = jnp.dot(ca, w_ref[0].astype(BF16), preferred_element_type=F32)
    o_ref[0] = acc + b_ref[0]


def _cond_matmul(c_pad, w, b, tn=512):
    nl, d, n = w.shape
    r = c_pad.shape[0]
    return pl.pallas_call(
        _cond_kernel,
        grid=(nl, n // tn),
        in_specs=[
            pl.BlockSpec((r, d), lambda l, j: (0, 0)),
            pl.BlockSpec((1, d, tn), lambda l, j: (l, 0, j)),
            pl.BlockSpec((1, 1, tn), lambda l, j: (l, 0, j)),
        ],
        out_specs=pl.BlockSpec((1, r, tn), lambda l, j: (l, 0, j)),
        out_shape=jax.ShapeDtypeStruct((nl, r, n), F32),
        compiler_params=_cparams(("parallel", "parallel")),
        name="cond_matmul",
    )(c_pad, w, b)


def _normmod_mm_kernel(x_ref, g_ref, sh_ref, sc_ref, w_ref, cs_ref, o_ref, h_ref, *, split):
    @pl.when(pl.program_id(1) == 0)
    def _():
        x = x_ref[...]
        y = x * lax.rsqrt(jnp.mean(x * x, axis=-1, keepdims=True) + EPS)
        y = y * g_ref[...]
        h_ref[...] = (y * (1.0 + sc_ref[0]) + sh_ref[0]).astype(BF16)

    acc = jnp.dot(h_ref[...], w_ref[...], preferred_element_type=F32) * cs_ref[...]
    if split:
        for s in range(o_ref.shape[0]):
            o_ref[s] = acc[:, s * LANES:(s + 1) * LANES].astype(o_ref.dtype)
    else:
        o_ref[...] = acc.astype(o_ref.dtype)


def _normmod_matmul(x2d, g, shift, scale, w, colscale, seq_len, *, tm, tn, split=False):
    m, d = x2d.shape
    n = w.shape[1]
    rows_per_batch = seq_len // tm
    if split:
        out_shape = jax.ShapeDtypeStruct((n // LANES, m, LANES), BF16)
        out_spec = pl.BlockSpec((tn // LANES, tm, LANES), lambda i, j: (j, i, 0))
    else:
        out_shape = jax.ShapeDtypeStruct((m, n), BF16)
        out_spec = pl.BlockSpec((tm, tn), lambda i, j: (i, j))
    return pl.pallas_call(
        functools.partial(_normmod_mm_kernel, split=split),
        grid=(m // tm, n // tn),
        in_specs=[
            pl.BlockSpec((tm, d), lambda i, j: (i, 0)),
            pl.BlockSpec((1, d), lambda i, j: (0, 0)),
            pl.BlockSpec((1, 1, d), lambda i, j: (i // rows_per_batch, 0, 0)),
            pl.BlockSpec((1, 1, d), lambda i, j: (i // rows_per_batch, 0, 0)),
            pl.BlockSpec((d, tn), lambda i, j: (0, j)),
            pl.BlockSpec((1, tn), lambda i, j: (0, j)),
        ],
        out_specs=out_spec,
        out_shape=out_shape,
        scratch_shapes=[pltpu.VMEM((tm, d), BF16)],
        compiler_params=_cparams(("parallel", "arbitrary")),
        name="normmod_matmul_split" if split else "normmod_matmul",
    )(x2d, g, shift, scale, w, colscale)


def _mm_res_kernel(a_ref, w_ref, x_ref, gate_ref, fg_ref, o_ref, *, final_norm):
    acc = jnp.dot(a_ref[...], w_ref[...], preferred_element_type=F32)
    xn = x_ref[...] + gate_ref[0] * acc
    if final_norm:
        xn = xn * lax.rsqrt(jnp.mean(xn * xn, axis=-1, keepdims=True) + EPS) * fg_ref[...]
    o_ref[...] = xn


def _mm_residual(a, w, x2d, gate, final_g, seq_len, *, tm, final_norm):
    m, k = a.shape
    n = w.shape[1]
    rows_per_batch = seq_len // tm
    return pl.pallas_call(
        functools.partial(_mm_res_kernel, final_norm=final_norm),
        grid=(m // tm,),
        in_specs=[
            pl.BlockSpec((tm, k), lambda i: (i, 0)),
            pl.BlockSpec((k, n), lambda i: (0, 0)),
            pl.BlockSpec((tm, n), lambda i: (i, 0)),
            pl.BlockSpec((1, 1, n), lambda i: (i // rows_per_batch, 0, 0)),
            pl.BlockSpec((1, n), lambda i: (0, 0)),
        ],
        out_specs=pl.BlockSpec((tm, n), lambda i: (i, 0)),
        out_shape=jax.ShapeDtypeStruct((m, n), F32),
        compiler_params=_cparams(("parallel",)),
        name="matmul_residual_final" if final_norm else "matmul_residual",
    )(a, w, x2d, gate, final_g)


def _glu_kernel(y_ref, z_ref, w_ref, b_ref, o_ref):
    y = y_ref[...]
    lin = jnp.dot(y, w_ref[...], preferred_element_type=F32) + b_ref[...]
    yf = y.astype(F32)
    z = z_ref[...].astype(F32)
    o_ref[...] = ((yf * _sigmoid(lin)) * (z * _sigmoid(z))).astype(o_ref.dtype)


def _glu(y, uz, w, b, *, tm):
    m, e = y.shape
    return pl.pallas_call(
        _glu_kernel,
        grid=(m // tm,),
        in_specs=[
            pl.BlockSpec((tm, e), lambda i: (i, 0)),
            pl.BlockSpec((tm, e), lambda i: (i, 1)),
            pl.BlockSpec((e, e), lambda i: (0, 0)),
            pl.BlockSpec((1, e), lambda i: (0, 0)),
        ],
        out_specs=pl.BlockSpec((tm, e), lambda i: (i, 0)),
        out_shape=jax.ShapeDtypeStruct((m, e), BF16),
        compiler_params=_cparams(("parallel",)),
        name="s5_glu",
    )(y, uz, w, b)


def _s5_scan_kernel(u_ref, wb_ref, wc_ref, are_ref, aim_ref, d_ref, y_ref, s_ref, st_ref, *, steps, nblk):
    nslab = STATES_PER_BLOCK // LANES

    @pl.when(pl.program_id(1) == 0)
    def _():
        st_ref[...] = jnp.zeros_like(st_ref)

    for blk in range(nblk):
        bu = jnp.dot(u_ref[:, blk * MXU_DIM:(blk + 1) * MXU_DIM], wb_ref[blk], preferred_element_type=F32)
        for k in range(2 * nslab):
            s_ref[k, pl.ds(blk, steps, stride=nblk), :] = bu[:, k * LANES:(k + 1) * LANES]

    a_re = [are_ref[:, k * LANES:(k + 1) * LANES] for k in range(nslab)]
    a_im = [aim_ref[:, k * LANES:(k + 1) * LANES] for k in range(nslab)]

    def step(t, carry):
        xr, xi = carry
        r0 = pl.multiple_of(t * nblk, nblk)
        nr, ni = [], []
        for k in range(nslab):
            b_re = s_ref[k, pl.ds(r0, nblk), :]
            b_im = s_ref[nslab + k, pl.ds(r0, nblk), :]
            v_re = a_re[k] * xr[k] - a_im[k] * xi[k] + b_re
            v_im = a_re[k] * xi[k] + a_im[k] * xr[k] + b_im
            s_ref[k, pl.ds(r0, nblk), :] = v_re
            s_ref[nslab + k, pl.ds(r0, nblk), :] = v_im
            nr.append(v_re)
            ni.append(v_im)
        return tuple(nr), tuple(ni)

    init = (tuple(st_ref[k] for k in range(nslab)), tuple(st_ref[nslab + k] for k in range(nslab)))
    xr, xi = lax.fori_loop(0, steps, step, init, unroll=2)
    for k in range(nslab):
        st_ref[k] = xr[k]
        st_ref[nslab + k] = xi[k]

    for blk in range(nblk):
        xs = jnp.concatenate(
            [s_ref[k, pl.ds(blk, steps, stride=nblk), :].astype(BF16) for k in range(2 * nslab)], axis=1)
        cols = slice(blk * MXU_DIM, (blk + 1) * MXU_DIM)
        yb = jnp.dot(xs, wc_ref[blk], preferred_element_type=F32)
        yb = yb + d_ref[:, cols] * u_ref[:, cols].astype(F32)
        y_ref[:, cols] = _gelu_tanh(yb).astype(y_ref.dtype)


def _s5_scan(uz, wb, wc, a_re, a_im, d_skip, batch, seq_len, *, steps):
    m = uz.shape[0]
    nblk, _, two_states = wb.shape
    e = nblk * MXU_DIM
    chunks = seq_len // steps
    nslab2 = two_states // LANES
    return pl.pallas_call(
        functools.partial(_s5_scan_kernel, steps=steps, nblk=nblk),
        grid=(batch, chunks),
        in_specs=[
            pl.BlockSpec((steps, e), lambda b, c: (b * chunks + c, 0)),
            pl.BlockSpec(wb.shape, lambda b, c: (0, 0, 0)),
            pl.BlockSpec(wc.shape, lambda b, c: (0, 0, 0)),
            pl.BlockSpec(a_re.shape, lambda b, c: (0, 0)),
            pl.BlockSpec(a_im.shape, lambda b, c: (0, 0)),
            pl.BlockSpec((1, e), lambda b, c: (0, 0)),
        ],
        out_specs=pl.BlockSpec((steps, e), lambda b, c: (b * chunks + c, 0)),
        out_shape=jax.ShapeDtypeStruct((m, e), BF16),
        scratch_shapes=[
            pltpu.VMEM((nslab2, steps * nblk, LANES), F32),
            pltpu.VMEM((nslab2, nblk, LANES), F32),
        ],
        compiler_params=_cparams(("parallel", "arbitrary")),
        name="s5_scan",
    )(uz, wb, wc, a_re, a_im, d_skip)


def _s5_params(lam_re, lam_im, log_step, b_re, b_im, c_re, c_im):
    g, n = lam_re.shape
    dt = jnp.exp(log_step.astype(F32))[:, None]
    lr, li = lam_re.astype(F32), lam_im.astype(F32)
    mag = jnp.exp(lr * dt)
    a_re, a_im = mag * jnp.cos(li * dt), mag * jnp.sin(li * dt)
    den = lr * lr + li * li
    coef_re = ((a_re - 1.0) * lr + a_im * li) / den
    coef_im = (a_im * lr - (a_re - 1.0) * li) / den
    br, bi = b_re.astype(F32), b_im.astype(F32)
    bb_re = coef_re[..., None] * br - coef_im[..., None] * bi
    bb_im = coef_re[..., None] * bi + coef_im[..., None] * br
    nblk = g // GROUPS_PER_BLOCK
    eye = jnp.eye(GROUPS_PER_BLOCK, dtype=F32)

    def blockdiag_in(bb):
        t = bb.reshape(nblk, GROUPS_PER_BLOCK, n, SSM_GROUP)
        t = jnp.einsum('bgnc,gh->bgchn', t, eye)
        return t.reshape(nblk, GROUPS_PER_BLOCK * SSM_GROUP, GROUPS_PER_BLOCK * n)

    def blockdiag_out(cc):
        t = cc.astype(F32).reshape(nblk, GROUPS_PER_BLOCK, SSM_GROUP, n)
        t = jnp.einsum('bgcn,gh->bgnhc', t, eye)
        return t.reshape(nblk, GROUPS_PER_BLOCK * n, GROUPS_PER_BLOCK * SSM_GROUP)

    wb = jnp.concatenate([blockdiag_in(bb_re), blockdiag_in(bb_im)], axis=2).astype(BF16)
    wc = jnp.concatenate([blockdiag_out(c_re), -blockdiag_out(c_im)], axis=1).astype(BF16)
    return wb, wc, a_re.reshape(nblk, GROUPS_PER_BLOCK * n), a_im.reshape(nblk, GROUPS_PER_BLOCK * n)


def _compress_kernel(x_ref, w1_ref, pe_ref, b1_ref, w2_ref, b2_ref, o_ref, sh_ref, *, n_cmp):
    nb = x_ref.shape[2]
    w1 = w1_ref[0]
    r = jnp.dot(x_ref[0, 0], w1, preferred_element_type=F32)
    pe = pe_ref[0]
    pe_hi = pe.astype(BF16)
    pe_lo = (pe - pe_hi.astype(F32)).astype(BF16)
    pw = (jnp.dot(pe_hi, w1, preferred_element_type=F32) + jnp.dot(pe_lo, w1, preferred_element_type=F32))
    const = pw[0:1, :HEAD_DIM] + pw[1:2, HEAD_DIM:] + b1_ref[0]
    sh_ref[pl.ds(0, nb), :] = r[:, HEAD_DIM:]
    sh_ref[pl.ds(nb, SUBLANES), :] = jnp.zeros((SUBLANES, HEAD_DIM), F32)
    hid = r[:, :HEAD_DIM] + sh_ref[pl.ds(1, nb), :] + const
    out = jnp.dot(_gelu_tanh(hid).astype(BF16), w2_ref[0], preferred_element_type=F32) + b2_ref[0]
    row = lax.broadcasted_iota(jnp.int32, out.shape, 0)
    o_ref[0, 0, 0] = jnp.where(row < n_cmp, out, 0.0).astype(o_ref.dtype)


def _compress(kv16, w1cat, pe2, b1, w2, b2, batch, n_cmp):
    nb, wide = kv16.shape[2], kv16.shape[3]
    return pl.pallas_call(
        functools.partial(_compress_kernel, n_cmp=n_cmp),
        grid=(2, N_KV, batch),
        in_specs=[
            pl.BlockSpec((1, 1, nb, wide), lambda s, g, b: (s * N_KV + g, b, 0, 0)),
            pl.BlockSpec((1, wide, 2 * HEAD_DIM), lambda s, g, b: (s, 0, 0)),
            pl.BlockSpec((1, 2 * SUBLANES, wide), lambda s, g, b: (s, 0, 0)),
            pl.BlockSpec((1, 1, HEAD_DIM), lambda s, g, b: (s, 0, 0)),
            pl.BlockSpec((1, HEAD_DIM, HEAD_DIM), lambda s, g, b: (s, 0, 0)),
            pl.BlockSpec((1, 1, HEAD_DIM), lambda s, g, b: (s, 0, 0)),
        ],
        out_specs=pl.BlockSpec((1, 1, 1, nb, HEAD_DIM), lambda s, g, b: (s, b, g, 0, 0)),
        out_shape=jax.ShapeDtypeStruct((2, batch, N_KV, nb, HEAD_DIM), BF16),
        scratch_shapes=[pltpu.VMEM((nb + SUBLANES, HEAD_DIM), F32)],
        compiler_params=_cparams(("parallel", "parallel", "parallel")),
        name="nsa_compress",
    )(kv16, w1cat, pe2, b1, w2, b2)


def _softmax_rows(s, mask):
    s = jnp.where(mask, s, NEG)
    m = jnp.max(s, axis=-1, keepdims=True)
    e = jnp.where(mask, jnp.exp(s - m), 0.0)
    l = jnp.sum(e, axis=-1, keepdims=True)
    return e / jnp.where(l > 0.0, l, 1.0)


def _nsa_kernel(q_ref, z0_ref, z1_ref, z2_ref, gt_ref, kc_ref, vc_ref, ka_ref, vs_ref, kw_ref, vw_ref,
                ovt_ref, o_ref, *, tq, n_cmp, n_slc, topk):
    t0 = pl.program_id(2) * tq
    rows = HPG * tq
    nsp = ovt_ref.shape[0]

    q = q_ref[...]
    qh = jnp.concatenate([q[:, h * HEAD_DIM:(h + 1) * HEAD_DIM] for h in range(HPG)], axis=0)

    def row_pos(shape):
        r = lax.broadcasted_iota(jnp.int32, shape, 0)
        return t0 + (r % tq if shape[0] > tq else r)

    ncp = kc_ref.shape[3]
    s = _dot_nt(qh, kc_ref[0, 0, 0])
    n_idx = lax.broadcasted_iota(jnp.int32, (rows, ncp), 1)
    cmask = (n_idx * CMP_STRIDE + (CMP_LEN - 1) <= row_pos((rows, ncp))) & (n_idx < n_cmp)
    p = _softmax_rows(s, cmask)
    o_cmp = jnp.dot(p.astype(BF16), vc_ref[0, 0, 0], preferred_element_type=F32)
    p_cmp = p[0:tq]
    for h in range(1, HPG):
        p_cmp = p_cmp + p[h * tq:(h + 1) * tq]

    p_hi = p_cmp.astype(BF16)
    r1 = p_cmp - p_hi.astype(F32)
    p_mid = r1.astype(BF16)
    p_lo = (r1 - p_mid.astype(F32)).astype(BF16)
    ovt = ovt_ref[...]
    p_slc = _dot_nt(ovt, p_hi) + _dot_nt(ovt, p_mid) + _dot_nt(ovt, p_lo)
    blk = lax.broadcasted_iota(jnp.int32, (nsp, tq), 0)
    cur = (t0 + lax.broadcasted_iota(jnp.int32, (nsp, tq), 1)) // SEL_LEN
    valid = (blk <= cur) & (blk < n_slc)
    forced = ((blk == 0) | (blk == cur) | (blk == cur - 1)).astype(F32)
    score = jnp.where(valid, p_slc + SEL_BONUS * forced, -SEL_BONUS)
    rank = jnp.zeros((nsp, tq), jnp.int32)
    for i in range(n_slc):
        si = jnp.broadcast_to(score[i:i + 1, :], (nsp, tq))
        beats = (si > score) | ((si == score) & (blk > i))
        rank = rank + jnp.where(beats, 1, 0)
    sel = (rank < topk) & (score > -0.5 * SEL_BONUS)
    selb = jnp.where(sel, 0.0, NEG).T.astype(BF16)
    pad = jnp.zeros((tq, ka_ref.shape[3] - HEAD_DIM - nsp), BF16)
    bias_cols = jnp.concatenate([selb, pad], axis=1)
    q_aug = jnp.concatenate([qh, jnp.concatenate([bias_cols] * HPG, axis=0)], axis=1)

    def flash_step(kt, carry, causal):
        m, l, acc = carry
        k0 = pl.multiple_of(kt * tq, tq)
        s = _dot_nt(q_aug, ka_ref[0, 0, pl.ds(k0, tq), :])
        if causal:
            kpos = k0 + lax.broadcasted_iota(jnp.int32, s.shape, 1)
            s = jnp.where(kpos <= row_pos(s.shape), s, NEG)
        m_new = jnp.maximum(m, jnp.max(s, axis=-1, keepdims=True))
        alpha = jnp.exp(m - m_new)
        pe = jnp.exp(s - m_new)
        l = alpha * l + jnp.sum(pe, axis=-1, keepdims=True)
        acc = alpha * acc + jnp.dot(pe.astype(BF16), vs_ref[0, 0, pl.ds(k0, tq), :], preferred_element_type=F32)
        return m_new, l, acc

    carry = (jnp.full((rows, 1), NEG, F32), jnp.zeros((rows, 1), F32), jnp.zeros((rows, HEAD_DIM), F32))
    qi = pl.program_id(2)
    carry = lax.fori_loop(0, qi, lambda kt, c: flash_step(kt, c, False), carry)
    _, l_sel, acc_sel = flash_step(qi, carry, True)
    o_sel = acc_sel / l_sel

    span = WINDOW + tq
    start = pl.multiple_of(jnp.maximum(t0 - WINDOW, 0), tq)
    s = _dot_nt(qh, kw_ref[0, 0, pl.ds(start, span), :])
    rel = row_pos((rows, span)) - (start + lax.broadcasted_iota(jnp.int32, (rows, span), 1))
    pw = _softmax_rows(s, (rel >= 0) & (rel < WINDOW))
    o_win = jnp.dot(pw.astype(BF16), vw_ref[0, 0, pl.ds(start, span), :], preferred_element_type=F32)

    gates = _sigmoid(gt_ref[...].astype(F32))
    for h in range(HPG):
        hs = slice(h * tq, (h + 1) * tq)
        cs = slice(h * HEAD_DIM, (h + 1) * HEAD_DIM)
        tot = None
        for br, (o_br, z_ref) in enumerate(((o_cmp, z0_ref), (o_sel, z1_ref), (o_win, z2_ref))):
            z = z_ref[:, cs].astype(F32)
            c = br * HPG + h
            term = (gates[:, c:c + 1] * (z * _sigmoid(z))) * o_br[hs]
            tot = term if tot is None else tot + term
        o_ref[:, cs] = tot.astype(o_ref.dtype)


def _nsa_attention(proj, kcv, kaug, kv, ovt, batch, seq_len, *, tq):
    m = proj.shape[0]
    gw = HPG * HEAD_DIM
    qt = seq_len // tq
    n_cmp = (seq_len - CMP_LEN) // CMP_STRIDE + 1
    n_slc = seq_len // SEL_LEN
    z_base = N_KV
    gate_base = (N_KV + N_BRANCH * N_KV) * (gw // LANES)
    ncp = kcv.shape[3]

    def zspec(br):
        return pl.BlockSpec((tq, gw), lambda b, g, i: (b * qt + i, z_base + br * N_KV + g))

    return pl.pallas_call(
        functools.partial(_nsa_kernel, tq=tq, n_cmp=n_cmp, n_slc=n_slc, topk=min(SEL_TOPK, n_slc)),
        grid=(batch, N_KV, qt),
        in_specs=[
            pl.BlockSpec((tq, gw), lambda b, g, i: (b * qt + i, g)),
            zspec(0), zspec(1), zspec(2),
            pl.BlockSpec((tq, LANES), lambda b, g, i: (b * qt + i, gate_base + g)),
            pl.BlockSpec((1, 1, 1, ncp, HEAD_DIM), lambda b, g, i: (0, b, g, 0, 0)),
            pl.BlockSpec((1, 1, 1, ncp, HEAD_DIM), lambda b, g, i: (1, b, g, 0, 0)),
            pl.BlockSpec((1, 1, seq_len, kaug.shape[3]), lambda b, g, i: (g, b, 0, 0)),
            pl.BlockSpec((1, 1, seq_len, HEAD_DIM), lambda b, g, i: (3 * N_KV + g, b, 0, 0)),
            pl.BlockSpec((1, 1, seq_len, HEAD_DIM), lambda b, g, i: (4 * N_KV + g, b, 0, 0)),
            pl.BlockSpec((1, 1, seq_len, HEAD_DIM), lambda b, g, i: (5 * N_KV + g, b, 0, 0)),
            pl.BlockSpec(ovt.shape, lambda b, g, i: (0, 0)),
        ],
        out_specs=pl.BlockSpec((tq, gw), lambda b, g, i: (b * qt + i, g)),
        out_shape=jax.ShapeDtypeStruct((m, N_KV * gw), BF16),
        compiler_params=_cparams(("parallel", "parallel", "arbitrary")),
        name="nsa_attention",
    )(proj, proj, proj, proj, proj, kcv, kcv, kaug, kv, kv, kv, ovt)


def _selection_overlap_t(n_cmp, ncp, n_slc, nsp):
    c0 = np.arange(ncp)[None, :] * CMP_STRIDE
    s0 = np.arange(nsp)[:, None] * SEL_LEN
    ov = np.clip(np.minimum(c0 + CMP_LEN, s0 + SEL_LEN) - np.maximum(c0, s0), 0, None) / CMP_STRIDE
    ov = ov * (np.arange(ncp)[None, :] < n_cmp) * (np.arange(nsp)[:, None] < n_slc)
    return jnp.asarray(ov, dtype=BF16)


def _s5_layer(x2d, mods, norm_g, w_in, s5p, d_skip, w_glu, b_glu, w_out, batch, seq_len, final_g, final_norm):
    d = x2d.shape[1]
    shift, scale, gate = (mods[:, None, i * d:(i + 1) * d] for i in range(3))
    e = w_glu.shape[0]
    uz = _normmod_matmul(x2d, norm_g[None], shift, scale, w_in.astype(BF16), jnp.ones((1, 2 * e), F32),
                         seq_len, tm=512, tn=1024)
    wb, wc, a_re, a_im = s5p
    y = _s5_scan(uz, wb, wc, a_re, a_im, d_skip[None].astype(F32), batch, seq_len, steps=128)
    t = _glu(y, uz, w_glu.astype(BF16), b_glu[None].astype(F32), tm=512)
    return _mm_residual(t, w_out.astype(BF16), x2d, gate, final_g[None], seq_len, tm=512, final_norm=final_norm)


def _qg_weight(w_qg):
    d = w_qg.shape[0]
    att = N_HEADS * HEAD_DIM
    g_end = att + N_BRANCH * N_HEADS
    wg = w_qg[:, att:g_end].reshape(d, N_BRANCH, N_KV, HPG)
    wg = jnp.transpose(wg, (0, 2, 1, 3)).reshape(d, N_KV, N_BRANCH * HPG)
    wg = jnp.pad(wg, ((0, 0), (0, 0), (0, LANES - N_BRANCH * HPG))).reshape(d, N_KV * LANES)
    w = jnp.concatenate([w_qg[:, :att], w_qg[:, g_end:], wg], axis=1).astype(BF16)
    cs = jnp.concatenate([jnp.full((att,), HEAD_DIM ** -0.5, F32), jnp.ones((w.shape[1] - att,), F32)])
    return w, cs[None]


def kernel(x, c, norm_g, mod_w, mod_b, ssm_w_in, ssm_lam_re, ssm_lam_im, ssm_log_step, ssm_b_re, ssm_b_im, ssm_c_re, ssm_c_im, ssm_d, ssm_w_glu, ssm_b_glu, ssm_w_out, kv_norm_g, kv_mod_w, kv_mod_b, w_kv, cmp_pe, cmp_w1, cmp_b1, cmp_w2, cmp_b2, nsa_w_qg, nsa_w_o, final_norm_g):
    batch, seq_len, d = x.shape
    depth = mod_w.shape[0]
    n_a = ssm_w_in.shape[0]
    m = batch * seq_len
    x2d = x.reshape(m, d)

    c_pad = jnp.pad(c, ((0, 2 * SUBLANES - batch), (0, 0)))
    mods = _cond_matmul(c_pad, mod_w, mod_b[:, None])[:, :batch]
    kv_mods = _cond_matmul(c_pad, kv_mod_w[None], kv_mod_b[None, None])[0, :batch]

    for layer in range(n_a):
        s5p = _s5_params(ssm_lam_re[layer], ssm_lam_im[layer], ssm_log_step[layer], ssm_b_re[layer],
                         ssm_b_im[layer], ssm_c_re[layer], ssm_c_im[layer])
        x2d = _s5_layer(x2d, mods[layer], norm_g[layer], ssm_w_in[layer], s5p, ssm_d[layer], ssm_w_glu[layer],
                        ssm_b_glu[layer], ssm_w_out[layer], batch, seq_len, final_norm_g,
                        final_norm=(layer == depth - 1))

    kv_shift, kv_scale = kv_mods[:, None, :d], kv_mods[:, None, d:]
    n_kvcols = w_kv.shape[1]
    kv = _normmod_matmul(x2d, kv_norm_g[None], kv_shift, kv_scale, w_kv.astype(BF16),
                         jnp.ones((1, n_kvcols), F32), seq_len, tm=512, tn=512, split=True)
    kv = kv.reshape(n_kvcols // HEAD_DIM, batch, seq_len, HEAD_DIM)

    n_cmp = (seq_len - CMP_LEN) // CMP_STRIDE + 1
    n_slc = seq_len // SEL_LEN
    nb16 = seq_len // CMP_STRIDE
    half = CMP_STRIDE * HEAD_DIM
    kv16 = kv.reshape(kv.shape[0], batch, nb16, half)
    w1cat = jnp.concatenate([cmp_w1[:, :half], cmp_w1[:, half:]], axis=2).astype(BF16)
    pe2 = jnp.pad(cmp_pe.reshape(2, 2, half), ((0, 0), (0, 2 * SUBLANES - 2), (0, 0)))
    kcv = _compress(kv16, w1cat, pe2, cmp_b1[:, None], cmp_w2.astype(BF16), cmp_b2[:, None], batch, n_cmp)

    nsp = HEAD_DIM // 2
    onehot = (jnp.arange(seq_len)[:, None] // SEL_LEN == jnp.arange(HEAD_DIM)[None, :]).astype(BF16)
    kaug = jnp.concatenate(
        [kv[2 * N_KV:3 * N_KV], jnp.broadcast_to(onehot, (N_KV, batch, seq_len, HEAD_DIM))], axis=-1)
    ovt = _selection_overlap_t(n_cmp, nb16, n_slc, nsp)

    for layer in range(n_a, depth):
        j = layer - n_a
        shift, scale, gate = (mods[layer][:, None, i * d:(i + 1) * d] for i in range(3))
        wq, cs = _qg_weight(nsa_w_qg[j])
        proj = _normmod_matmul(x2d, norm_g[layer][None], shift, scale, wq, cs, seq_len, tm=512, tn=512)
        o = _nsa_attention(proj, kcv, kaug, kv, ovt, batch, seq_len, tq=128)
        x2d = _mm_residual(o, nsa_w_o[j].astype(BF16), x2d, gate, final_norm_g[None], seq_len, tm=512,
                           final_norm=(layer == depth - 1))

    return x2d.reshape(batch, seq_len, d)
```

```python
import functools
import math

import jax
import jax.numpy as jnp
import numpy as np
from jax import lax
from jax.experimental import pallas as pl
from jax.experimental.pallas import tpu as pltpu

F32 = jnp.float32
BF16 = jnp.bfloat16

SSM_GROUP = 16
SSM_STATE = 64
N_HEADS = 16
N_KV = 4
HPG = N_HEADS // N_KV
HEAD_DIM = 128
N_BRANCH = 3
CMP_LEN = 32
CMP_STRIDE = 16
SEL_LEN = 64
SEL_TOPK = 16
WINDOW = 512
SEL_BONUS = 1e3
NEG = -1e30
EPS = 1e-6

LANES = 128
SUBLANES = 8
MXU_DIM = 256
VMEM_LIMIT = 56 * 1024 * 1024

PROJ_ROWS = 1024

GROUPS_PER_BLOCK = MXU_DIM // SSM_GROUP
STATES_PER_BLOCK = GROUPS_PER_BLOCK * SSM_STATE


def _cparams(sem):
    return pltpu.CompilerParams(dimension_semantics=sem, vmem_limit_bytes=VMEM_LIMIT)


def _gelu_tanh(x):
    return x * (0.5 * (1.0 + jnp.tanh(math.sqrt(2.0 / math.pi) * (x + 0.044715 * (x * x * x)))))


def _sigmoid(x):
    return 1.0 / (1.0 + jnp.exp(-x))


def _dot_nt(a, b):
    return lax.dot_general(a, b, (((1,), (1,)), ((), ())), preferred_element_type=F32)


def _cond_kernel(c_ref, w_ref, b_ref, o_ref):
    c = c_ref[...]
    ca = (c * _sigmoid(c)).astype(BF16)
    acc = jnp.dot(ca, w_ref[0].astype(BF16), preferred_element_type=F32)
    o_ref[0] = acc + b_ref[0]


def _cond_matmul(c_pad, w, b, tn=512):
    nl, d, n = w.shape
    r = c_pad.shape[0]
    return pl.pallas_call(
        _cond_kernel,
        grid=(nl, n // tn),
        in_specs=[
            pl.BlockSpec((r, d), lambda l, j: (0, 0)),
            pl.BlockSpec((1, d, tn), lambda l, j: (l, 0, j)),
            pl.BlockSpec((1, 1, tn), lambda l, j: (l, 0, j)),
        ],
        out_specs=pl.BlockSpec((1, r, tn), lambda l, j: (l, 0, j)),
        out_shape=jax.ShapeDtypeStruct((nl, r, n), F32),
        compiler_params=_cparams(("parallel", "parallel")),
        name="cond_matmul",
    )(c_pad, w, b)


def _normmod_mm_kernel(x_ref, g_ref, sh_ref, sc_ref, w_ref, cs_ref, o_ref, h_ref, *, split):
    @pl.when(pl.program_id(1) == 0)
    def _():
        x = x_ref[...]
        y = x * lax.rsqrt(jnp.mean(x * x, axis=-1, keepdims=True) + EPS)
        y = y * g_ref[...]
        h_ref[...] = (y * (1.0 + sc_ref[0]) + sh_ref[0]).astype(BF16)

    acc = jnp.dot(h_ref[...], w_ref[...], preferred_element_type=F32) * cs_ref[...]
    if split:
        for s in range(o_ref.shape[0]):
            o_ref[s] = acc[:, s * LANES:(s + 1) * LANES].astype(o_ref.dtype)
    else:
        o_ref[...] = acc.astype(o_ref.dtype)


def _normmod_matmul(x2d, g, shift, scale, w, colscale, seq_len, *, tm, tn, split=False):
    m, d = x2d.shape
    n = w.shape[1]
    rows_per_batch = seq_len // tm
    if split:
        out_shape = jax.ShapeDtypeStruct((n // LANES, m, LANES), BF16)
        out_spec = pl.BlockSpec((tn // LANES, tm, LANES), lambda i, j: (j, i, 0))
    else:
        out_shape = jax.ShapeDtypeStruct((m, n), BF16)
        out_spec = pl.BlockSpec((tm, tn), lambda i, j: (i, j))
    return pl.pallas_call(
        functools.partial(_normmod_mm_kernel, split=split),
        grid=(m // tm, n // tn),
        in_specs=[
            pl.BlockSpec((tm, d), lambda i, j: (i, 0)),
            pl.BlockSpec((1, d), lambda i, j: (0, 0)),
            pl.BlockSpec((1, 1, d), lambda i, j: (i // rows_per_batch, 0, 0)),
            pl.BlockSpec((1, 1, d), lambda i, j: (i // rows_per_batch, 0, 0)),
            pl.BlockSpec((d, tn), lambda i, j: (0, j)),
            pl.BlockSpec((1, tn), lambda i, j: (0, j)),
        ],
        out_specs=out_spec,
        out_shape=out_shape,
        scratch_shapes=[pltpu.VMEM((tm, d), BF16)],
        compiler_params=_cparams(("parallel", "arbitrary")),
        name="normmod_matmul_split" if split else "normmod_matmul",
    )(x2d, g, shift, scale, w, colscale)


def _mm_res_kernel(a_ref, w_ref, x_ref, gate_ref, fg_ref, o_ref, *, final_norm):
    acc = jnp.dot(a_ref[...], w_ref[...], preferred_element_type=F32)
    xn = x_ref[...] + gate_ref[0] * acc
    if final_norm:
        xn = xn * lax.rsqrt(jnp.mean(xn * xn, axis=-1, keepdims=True) + EPS) * fg_ref[...]
    o_ref[...] = xn


def _mm_residual(a, w, x2d, gate, final_g, seq_len, *, tm, final_norm):
    m, k = a.shape
    n = w.shape[1]
    rows_per_batch = seq_len // tm
    return pl.pallas_call(
        functools.partial(_mm_res_kernel, final_norm=final_norm),
        grid=(m // tm,),
        in_specs=[
            pl.BlockSpec((tm, k), lambda i: (i, 0)),
            pl.BlockSpec((k, n), lambda i: (0, 0)),
            pl.BlockSpec((tm, n), lambda i: (i, 0)),
            pl.BlockSpec((1, 1, n), lambda i: (i // rows_per_batch, 0, 0)),
            pl.BlockSpec((1, n), lambda i: (0, 0)),
        ],
        out_specs=pl.BlockSpec((tm, n), lambda i: (i, 0)),
        out_shape=jax.ShapeDtypeStruct((m, n), F32),
        compiler_params=_cparams(("parallel",)),
        name="matmul_residual_final" if final_norm else "matmul_residual",
    )(a, w, x2d, gate, final_g)


def _glu_kernel(y_ref, z_ref, w_ref, b_ref, o_ref):
    y = y_ref[...]
    lin = jnp.dot(y, w_ref[...], preferred_element_type=F32) + b_ref[...]
    yf = y.astype(F32)
    z = z_ref[...].astype(F32)
    o_ref[...] = ((yf * _sigmoid(lin)) * (z * _sigmoid(z))).astype(o_ref.dtype)


def _glu(y, uz, w, b, *, tm):
    m, e = y.shape
    return pl.pallas_call(
        _glu_kernel,
        grid=(m // tm,),
        in_specs=[
            pl.BlockSpec((tm, e), lambda i: (i, 0)),
            pl.BlockSpec((tm, e), lambda i: (i, 1)),
            pl.BlockSpec((e, e), lambda i: (0, 0)),
            pl.BlockSpec((1, e), lambda i: (0, 0)),
        ],
        out_specs=pl.BlockSpec((tm, e), lambda i: (i, 0)),
        out_shape=jax.ShapeDtypeStruct((m, e), BF16),
        compiler_params=_cparams(("parallel",)),
        name="s5_glu",
    )(y, uz, w, b)


def _s5_scan_kernel(u_ref, wb_ref, wc_ref, are_ref, aim_ref, d_ref, y_ref, s_ref, st_ref, *, steps, nblk):
    nslab = STATES_PER_BLOCK // LANES

    @pl.when(pl.program_id(1) == 0)
    def _():
        st_ref[...] = jnp.zeros_like(st_ref)

    for blk in range(nblk):
        bu = jnp.dot(u_ref[:, blk * MXU_DIM:(blk + 1) * MXU_DIM], wb_ref[blk], preferred_element_type=F32)
        for k in range(2 * nslab):
            s_ref[k, pl.ds(blk, steps, stride=nblk), :] = bu[:, k * LANES:(k + 1) * LANES]

    a_re = [are_ref[:, k * LANES:(k + 1) * LANES] for k in range(nslab)]
    a_im = [aim_ref[:, k * LANES:(k + 1) * LANES] for k in range(nslab)]

    def step(t, carry):
        xr, xi = carry
        r0 = pl.multiple_of(t * nblk, nblk)
        nr, ni = [], []
        for k in range(nslab):
            b_re = s_ref[k, pl.ds(r0, nblk), :]
            b_im = s_ref[nslab + k, pl.ds(r0, nblk), :]
            v_re = a_re[k] * xr[k] - a_im[k] * xi[k] + b_re
            v_im = a_re[k] * xi[k] + a_im[k] * xr[k] + b_im
            s_ref[k, pl.ds(r0, nblk), :] = v_re
            s_ref[nslab + k, pl.ds(r0, nblk), :] = v_im
            nr.append(v_re)
            ni.append(v_im)
        return tuple(nr), tuple(ni)

    init = (tuple(st_ref[k] for k in range(nslab)), tuple(st_ref[nslab + k] for k in range(nslab)))
    xr, xi = lax.fori_loop(0, steps, step, init, unroll=2)
    for k in range(nslab):
        st_ref[k] = xr[k]
        st_ref[nslab + k] = xi[k]

    for blk in range(nblk):
        xs = jnp.concatenate(
            [s_ref[k, pl.ds(blk, steps, stride=nblk), :].astype(BF16) for k in range(2 * nslab)], axis=1)
        cols = slice(blk * MXU_DIM, (blk + 1) * MXU_DIM)
        yb = jnp.dot(xs, wc_ref[blk], preferred_element_type=F32)
        yb = yb + d_ref[:, cols] * u_ref[:, cols].astype(F32)
        y_ref[:, cols] = _gelu_tanh(yb).astype(y_ref.dtype)


def _s5_scan(uz, wb, wc, a_re, a_im, d_skip, batch, seq_len, *, steps):
    m = uz.shape[0]
    nblk, _, two_states = wb.shape
    e = nblk * MXU_DIM
    chunks = seq_len // steps
    nslab2 = two_states // LANES
    return pl.pallas_call(
        functools.partial(_s5_scan_kernel, steps=steps, nblk=nblk),
        grid=(batch, chunks),
        in_specs=[
            pl.BlockSpec((steps, e), lambda b, c: (b * chunks + c, 0)),
            pl.BlockSpec(wb.shape, lambda b, c: (0, 0, 0)),
            pl.BlockSpec(wc.shape, lambda b, c: (0, 0, 0)),
            pl.BlockSpec(a_re.shape, lambda b, c: (0, 0)),
            pl.BlockSpec(a_im.shape, lambda b, c: (0, 0)),
            pl.BlockSpec((1, e), lambda b, c: (0, 0)),
        ],
        out_specs=pl.BlockSpec((steps, e), lambda b, c: (b * chunks + c, 0)),
        out_shape=jax.ShapeDtypeStruct((m, e), BF16),
        scratch_shapes=[
            pltpu.VMEM((nslab2, steps * nblk, LANES), F32),
            pltpu.VMEM((nslab2, nblk, LANES), F32),
        ],
        compiler_params=_cparams(("parallel", "arbitrary")),
        name="s5_scan",
    )(uz, wb, wc, a_re, a_im, d_skip)


def _s5_params(lam_re, lam_im, log_step, b_re, b_im, c_re, c_im):
    g, n = lam_re.shape
    dt = jnp.exp(log_step.astype(F32))[:, None]
    lr, li = lam_re.astype(F32), lam_im.astype(F32)
    mag = jnp.exp(lr * dt)
    a_re, a_im = mag * jnp.cos(li * dt), mag * jnp.sin(li * dt)
    den = lr * lr + li * li
    coef_re = ((a_re - 1.0) * lr + a_im * li) / den
    coef_im = (a_im * lr - (a_re - 1.0) * li) / den
    br, bi = b_re.astype(F32), b_im.astype(F32)
    bb_re = coef_re[..., None] * br - coef_im[..., None] * bi
    bb_im = coef_re[..., None] * bi + coef_im[..., None] * br
    nblk = g // GROUPS_PER_BLOCK
    eye = jnp.eye(GROUPS_PER_BLOCK, dtype=F32)

    def blockdiag_in(bb):
        t = bb.reshape(nblk, GROUPS_PER_BLOCK, n, SSM_GROUP)
        t = jnp.einsum('bgnc,gh->bgchn', t, eye)
        return t.reshape(nblk, GROUPS_PER_BLOCK * SSM_GROUP, GROUPS_PER_BLOCK * n)

    def blockdiag_out(cc):
        t = cc.astype(F32).reshape(nblk, GROUPS_PER_BLOCK, SSM_GROUP, n)
        t = jnp.einsum('bgcn,gh->bgnhc', t, eye)
        return t.reshape(nblk, GROUPS_PER_BLOCK * n, GROUPS_PER_BLOCK * SSM_GROUP)

    wb = jnp.concatenate([blockdiag_in(bb_re), blockdiag_in(bb_im)], axis=2).astype(BF16)
    wc = jnp.concatenate([blockdiag_out(c_re), -blockdiag_out(c_im)], axis=1).astype(BF16)
    return wb, wc, a_re.reshape(nblk, GROUPS_PER_BLOCK * n), a_im.reshape(nblk, GROUPS_PER_BLOCK * n)


def _compress_kernel(x_ref, w1_ref, pe_ref, b1_ref, w2_ref, b2_ref, o_ref, sh_ref, *, n_cmp):
    nb = x_ref.shape[2]
    w1 = w1_ref[0]
    r = jnp.dot(x_ref[0, 0], w1, preferred_element_type=F32)
    pe = pe_ref[0]
    pe_hi = pe.astype(BF16)
    pe_lo = (pe - pe_hi.astype(F32)).astype(BF16)
    pw = (jnp.dot(pe_hi, w1, preferred_element_type=F32) + jnp.dot(pe_lo, w1, preferred_element_type=F32))
    const = pw[0:1, :HEAD_DIM] + pw[1:2, HEAD_DIM:] + b1_ref[0]
    sh_ref[pl.ds(0, nb), :] = r[:, HEAD_DIM:]
    sh_ref[pl.ds(nb, SUBLANES), :] = jnp.zeros((SUBLANES, HEAD_DIM), F32)
    hid = r[:, :HEAD_DIM] + sh_ref[pl.ds(1, nb), :] + const
    out = jnp.dot(_gelu_tanh(hid).astype(BF16), w2_ref[0], preferred_element_type=F32) + b2_ref[0]
    row = lax.broadcasted_iota(jnp.int32, out.shape, 0)
    o_ref[0, 0, 0] = jnp.where(row < n_cmp, out, 0.0).astype(o_ref.dtype)


def _compress(kv16, w1cat, pe2, b1, w2, b2, batch, n_cmp):
    nb, wide = kv16.shape[2], kv16.shape[3]
    return pl.pallas_call(
        functools.partial(_compress_kernel, n_cmp=n_cmp),
        grid=(2, N_KV, batch),
        in_specs=[
            pl.BlockSpec((1, 1, nb, wide), lambda s, g, b: (s * N_KV + g, b, 0, 0)),
            pl.BlockSpec((1, wide, 2 * HEAD_DIM), lambda s, g, b: (s, 0, 0)),
            pl.BlockSpec((1, 2 * SUBLANES, wide), lambda s, g, b: (s, 0, 0)),
            pl.BlockSpec((1, 1, HEAD_DIM), lambda s, g, b: (s, 0, 0)),
            pl.BlockSpec((1, HEAD_DIM, HEAD_DIM), lambda s, g, b: (s, 0, 0)),
            pl.BlockSpec((1, 1, HEAD_DIM), lambda s, g, b: (s, 0, 0)),
        ],
        out_specs=pl.BlockSpec((1, 1, 1, nb, HEAD_DIM), lambda s, g, b: (s, b, g, 0, 0)),
        out_shape=jax.ShapeDtypeStruct((2, batch, N_KV, nb, HEAD_DIM), BF16),
        scratch_shapes=[pltpu.VMEM((nb + SUBLANES, HEAD_DIM), F32)],
        compiler_params=_cparams(("parallel", "parallel", "parallel")),
        name="nsa_compress",
    )(kv16, w1cat, pe2, b1, w2, b2)


FLAG_COL =HEAD_DIM + HEAD_DIM // 2
SEL_KEY_TILE = 512


def _nsa_t_kernel(q_ref, z0_ref, z1_ref, z2_ref, gt_ref, kc_ref, vct_ref, ka_ref, vst_ref, kwa_ref, vwt_ref,
                  ovt_ref, o_ref, qa_ref, qw_ref, m_ref, l_ref, acc_ref, ot_ref, s_ref, gz_ref, *, tq, n_cmp, n_slc, topk):
    qi = pl.program_id(2)
    t0 = qi * tq
    rows = HPG * tq
    nsp = ovt_ref.shape[0]

    aug = ka_ref.shape[3]

    q = q_ref[...]
    qh = jnp.concatenate([q[:, h * HEAD_DIM:(h + 1) * HEAD_DIM] for h in range(HPG)], axis=0)
    flag = jnp.where(lax.broadcasted_iota(jnp.int32, (tq, aug - FLAG_COL), 1) == 0, NEG, 0.0).astype(BF16)
    qa_ref[:, 0:HEAD_DIM] = qh
    qw_ref[:, 0:HEAD_DIM] = qh
    qw_ref[:, HEAD_DIM:FLAG_COL] = jnp.zeros((rows, FLAG_COL - HEAD_DIM), BF16)
    for h in range(HPG):
        qa_ref[h * tq:(h + 1) * tq, FLAG_COL:] = flag
        qw_ref[h * tq:(h + 1) * tq, FLAG_COL:] = flag

    krow = lax.broadcasted_iota(jnp.int32, (tq, rows), 0)
    tcol = lax.broadcasted_iota(jnp.int32, (tq, rows), 1) % tq
    n_win = WINDOW // tq
    sw = _dot_nt(kwa_ref[0, 0, pl.ds(pl.multiple_of(t0, tq), WINDOW + tq), :], qw_ref[...])
    pieces = [sw[dd * tq:(dd + 1) * tq] for dd in range(n_win + 1)]
    pieces[0] = jnp.where(krow > tcol, pieces[0], NEG)
    pieces[n_win] = jnp.where(krow <= tcol, pieces[n_win], NEG)
    m_w = functools.reduce(jnp.maximum, [jnp.max(s, axis=0, keepdims=True) for s in pieces])
    l_w = jnp.zeros((1, rows), F32)
    o_w = jnp.zeros((HEAD_DIM, rows), F32)
    for dd, s in enumerate(pieces):
        pt = jnp.exp(s - m_w)
        l_w = l_w + jnp.sum(pt, axis=0, keepdims=True)
        o_w = o_w + jnp.dot(vwt_ref[0, 0, qi + dd], pt.astype(BF16), preferred_element_type=F32)
    ot_ref[2] = o_w / l_w

    gates = _sigmoid(gt_ref[...].astype(F32))
    for br, z_ref in enumerate((z0_ref, z1_ref, z2_ref)):
        for h in range(HPG):
            cs = slice(h * HEAD_DIM, (h + 1) * HEAD_DIM)
            z = z_ref[:, cs].astype(F32)
            c = br * HPG + h
            gz_ref[br, :, cs] = gates[:, c:c + 1] * (z * _sigmoid(z))

    ncp = kc_ref.shape[3]
    st = _dot_nt(kc_ref[0, 0, 0], qh)
    n_idx = lax.broadcasted_iota(jnp.int32, (ncp, rows), 0)
    tpos = t0 + lax.broadcasted_iota(jnp.int32, (ncp, rows), 1) % tq
    cmask = (n_idx * CMP_STRIDE + (CMP_LEN - 1) <= tpos) & (n_idx < n_cmp)
    st = jnp.where(cmask, st, NEG)
    e = jnp.where(cmask, jnp.exp(st - jnp.max(st, axis=0, keepdims=True)), 0.0)
    l = jnp.sum(e, axis=0, keepdims=True)
    p = e / jnp.where(l > 0.0, l, 1.0)
    ot_ref[0] = jnp.dot(vct_ref[0, 0], p.astype(BF16), preferred_element_type=F32)
    p_cmp = p[:, 0:tq]
    for h in range(1, HPG):
        p_cmp = p_cmp + p[:, h * tq:(h + 1) * tq]

    p_hi = p_cmp.astype(BF16)
    r1 = p_cmp - p_hi.astype(F32)
    p_mid = r1.astype(BF16)
    p_lo = (r1 - p_mid.astype(F32)).astype(BF16)
    ovt = ovt_ref[...]
    p_slc = (jnp.dot(ovt, p_hi, preferred_element_type=F32) + jnp.dot(ovt, p_mid, preferred_element_type=F32)
             + jnp.dot(ovt, p_lo, preferred_element_type=F32))
    blk = lax.broadcasted_iota(jnp.int32, (nsp, tq), 0)
    cur = (t0 + lax.broadcasted_iota(jnp.int32, (nsp, tq), 1)) // SEL_LEN
    valid = (blk <= cur) & (blk < n_slc)
    forced = (blk == 0) | (blk == cur) | (blk == cur - 1)
    score = jnp.where(valid, p_slc + jnp.where(forced, SEL_BONUS, 0.0), -SEL_BONUS)
    nv = nsp // SUBLANES
    sc = [score[k * SUBLANES:(k + 1) * SUBLANES] for k in range(nv)]
    sub = lax.broadcasted_iota(jnp.int32, (SUBLANES, tq), 0)
    rank = [jnp.zeros((SUBLANES, tq), jnp.int32) for _ in range(nv)]
    for i in range(n_slc):
        si = jnp.broadcast_to(score[i:i + 1, :], (SUBLANES, tq))
        for k in range(nv):
            if k * SUBLANES > i:
                beats = si >= sc[k]
            elif k * SUBLANES + SUBLANES - 1 <= i:
                beats = si > sc[k]
            else:
                beats = (si > sc[k]) | ((si == sc[k]) & (sub > i - k * SUBLANES))
            rank[k] = rank[k] + jnp.where(beats, 1, 0)
    rank = jnp.concatenate(rank, axis=0)
    sel = (rank < topk) & (score > -0.5 * SEL_BONUS)
    selb = jnp.where(sel, 0.0, NEG).T.astype(BF16)
    for h in range(HPG):
        qa_ref[h * tq:(h + 1) * tq, HEAD_DIM:FLAG_COL] = selb

    tk = s_ref.shape[0]

    def scores(k_tile):
        return _dot_nt(k_tile, qa_ref[...])

    def online_update(s, vt_tile):
        m_old = m_ref[...]
        m_new = jnp.maximum(m_old, jnp.max(s, axis=0, keepdims=True))
        alpha = jnp.exp(m_old - m_new)
        pt = jnp.exp(s - m_new)
        l_ref[...] = alpha * l_ref[...] + jnp.sum(pt, axis=0, keepdims=True)
        acc_ref[...] = alpha * acc_ref[...] + jnp.dot(vt_tile, pt.astype(BF16), preferred_element_type=F32)
        m_ref[...] = m_new

    m_ref[...] = jnp.full(m_ref.shape, NEG, F32)
    l_ref[...] = jnp.zeros(l_ref.shape, F32)
    acc_ref[...] = jnp.zeros(acc_ref.shape, F32)
    n_full = t0 // tk
    s_ref[...] = scores(ka_ref[0, 0, pl.ds(0, tk), :])

    def sel_body(kt, carry):
        k1 = pl.multiple_of((kt + 1) * tk, tk)
        s_next = scores(ka_ref[0, 0, pl.ds(k1, tk), :])
        online_update(s_ref[...], vst_ref[0, 0, kt])
        s_ref[...] = s_next
        return carry

    lax.fori_loop(0, n_full, sel_body, 0)
    kpos = n_full * tk + lax.broadcasted_iota(jnp.int32, (tk, rows), 0)
    tpos_k = t0 + lax.broadcasted_iota(jnp.int32, (tk, rows), 1) % tq
    online_update(jnp.where(kpos <= tpos_k, s_ref[...], NEG), vst_ref[0, 0, n_full])
    ot_ref[1] = acc_ref[...] / l_ref[...]

    for h in range(HPG):
        cs = slice(h * HEAD_DIM, (h + 1) * HEAD_DIM)
        tot = None
        for br in range(N_BRANCH):
            term = gz_ref[br, :, cs] * ot_ref[br, :, h * tq:(h + 1) * tq].T
            tot = term if tot is None else tot + term
        o_ref[:, cs] = tot.astype(o_ref.dtype)


def _nsa_attention_t(proj, kcv, vct, kaug, vst, kwa, vwt, ovt, batch, seq_len, *, tq):
    m = proj.shape[0]
    gw = HPG * HEAD_DIM
    qt = seq_len // tq
    n_cmp = (seq_len - CMP_LEN) // CMP_STRIDE + 1
    n_slc = seq_len // SEL_LEN
    z_base = N_KV
    gate_base = (N_KV + N_BRANCH * N_KV) * (gw // LANES)
    ncp = kcv.shape[3]
    aug = kaug.shape[3]
    rows = HPG * tq

    def zspec(br):
        return pl.BlockSpec((tq, gw), lambda b, g, i: (b * qt + i, z_base + br * N_KV + g))

    def per_group(arr):
        return pl.BlockSpec((1, 1) + arr.shape[2:], lambda b, g, i: (g, b) + (0,) * (arr.ndim - 2))

    return pl.pallas_call(
        functools.partial(_nsa_t_kernel, tq=tq, n_cmp=n_cmp, n_slc=n_slc, topk=min(SEL_TOPK, n_slc)),
        grid=(batch, N_KV, qt),
        in_specs=[
            pl.BlockSpec((tq, gw), lambda b, g, i: (b * qt + i, g)),
            zspec(0), zspec(1), zspec(2),
            pl.BlockSpec((tq, LANES), lambda b, g, i: (b * qt + i, gate_base + g)),
            pl.BlockSpec((1, 1, 1, ncp, HEAD_DIM), lambda b, g, i: (0, b, g, 0, 0)),
            pl.BlockSpec((1, 1, HEAD_DIM, ncp), lambda b, g, i: (b, g, 0, 0)),
            per_group(kaug), per_group(vst), per_group(kwa), per_group(vwt),
            pl.BlockSpec(ovt.shape, lambda b, g, i: (0, 0)),
        ],
        out_specs=pl.BlockSpec((tq, gw), lambda b, g, i: (b * qt + i, g)),
        out_shape=jax.ShapeDtypeStruct((m, N_KV * gw), BF16),
        scratch_shapes=[
            pltpu.VMEM((rows, aug), BF16),
            pltpu.VMEM((rows, aug), BF16),
            pltpu.VMEM((1, rows), F32),
            pltpu.VMEM((1, rows), F32),
            pltpu.VMEM((HEAD_DIM, rows), F32),
            pltpu.VMEM((N_BRANCH, HEAD_DIM, rows), F32),
            pltpu.VMEM((vst.shape[-1], rows), F32),
            pltpu.VMEM((N_BRANCH, tq, gw), F32),
        ],
        compiler_params=_cparams(("parallel", "parallel", "arbitrary")),
        name="nsa_attention",
    )(proj, proj, proj, proj, proj, kcv, vct, kaug, vst, kwa, vwt, ovt)


def _key_tiles_t(v, tq):
    lead = v.shape[:-2]
    nt = v.shape[-2] // tq
    return jnp.swapaxes(v.reshape(lead + (nt, tq, v.shape[-1])), -1, -2)


def _selection_overlap_t(n_cmp, ncp, n_slc, nsp):
    c0 = np.arange(ncp)[None, :] * CMP_STRIDE
    s0 = np.arange(nsp)[:, None] * SEL_LEN
    ov = np.clip(np.minimum(c0 + CMP_LEN, s0 + SEL_LEN) - np.maximum(c0, s0), 0, None) / CMP_STRIDE
    ov = ov * (np.arange(ncp)[None, :] < n_cmp) * (np.arange(nsp)[:, None] < n_slc)
    return jnp.asarray(ov, dtype=BF16)


def _s5_layer(x2d, mods, norm_g, w_in, s5p, d_skip, w_glu, b_glu, w_out, batch, seq_len, final_g, final_norm):
    d = x2d.shape[1]
    shift, scale, gate = (mods[:, None, i * d:(i + 1) * d] for i in range(3))
    e = w_glu.shape[0]
    uz = _normmod_matmul(x2d, norm_g[None], shift, scale, w_in.astype(BF16), jnp.ones((1, 2 * e), F32),
                         seq_len, tm=PROJ_ROWS, tn=1024)
    wb, wc, a_re, a_im = s5p
    y = _s5_scan(uz, wb, wc, a_re, a_im, d_skip[None].astype(F32), batch, seq_len, steps=256)
    t = _glu(y, uz, w_glu.astype(BF16), b_glu[None].astype(F32), tm=512)
    return _mm_residual(t, w_out.astype(BF16), x2d, gate, final_g[None], seq_len, tm=512, final_norm=final_norm)


def _qg_weight(w_qg):
    d = w_qg.shape[0]
    att = N_HEADS * HEAD_DIM
    g_end = att + N_BRANCH * N_HEADS
    wg = w_qg[:, att:g_end].reshape(d, N_BRANCH, N_KV, HPG)
    wg = jnp.transpose(wg, (0, 2, 1, 3)).reshape(d, N_KV, N_BRANCH * HPG)
    wg = jnp.pad(wg, ((0, 0), (0, 0), (0, LANES - N_BRANCH * HPG))).reshape(d, N_KV * LANES)
    w = jnp.concatenate([w_qg[:, :att], w_qg[:, g_end:], wg], axis=1).astype(BF16)
    cs = jnp.concatenate([jnp.full((att,), HEAD_DIM ** -0.5, F32), jnp.ones((w.shape[1] - att,), F32)])
    return w, cs[None]


def kernel(x, c, norm_g, mod_w, mod_b, ssm_w_in, ssm_lam_re, ssm_lam_im, ssm_log_step, ssm_b_re, ssm_b_im, ssm_c_re, ssm_c_im, ssm_d, ssm_w_glu, ssm_b_glu, ssm_w_out, kv_norm_g, kv_mod_w, kv_mod_b, w_kv, cmp_pe, cmp_w1, cmp_b1, cmp_w2, cmp_b2, nsa_w_qg, nsa_w_o, final_norm_g):
    batch, seq_len, d = x.shape
    depth = mod_w.shape[0]
    n_a = ssm_w_in.shape[0]
    m = batch * seq_len
    x2d = x.reshape(m, d)

    c_pad = jnp.pad(c, ((0, 2 * SUBLANES - batch), (0, 0)))
    mods = _cond_matmul(c_pad, mod_w, mod_b[:, None])[:, :batch]
    kv_mods = _cond_matmul(c_pad, kv_mod_w[None], kv_mod_b[None, None])[0, :batch]

    for layer in range(n_a):
        s5p = _s5_params(ssm_lam_re[layer], ssm_lam_im[layer], ssm_log_step[layer], ssm_b_re[layer],
                         ssm_b_im[layer], ssm_c_re[layer], ssm_c_im[layer])
        x2d = _s5_layer(x2d, mods[layer], norm_g[layer], ssm_w_in[layer], s5p, ssm_d[layer], ssm_w_glu[layer],
                        ssm_b_glu[layer], ssm_w_out[layer], batch, seq_len, final_norm_g,
                        final_norm=(layer == depth - 1))

    kv_shift, kv_scale = kv_mods[:, None, :d], kv_mods[:, None, d:]
    n_kvcols = w_kv.shape[1]
    kv = _normmod_matmul(x2d, kv_norm_g[None], kv_shift, kv_scale, w_kv.astype(BF16),
                         jnp.ones((1, n_kvcols), F32), seq_len, tm=PROJ_ROWS, tn=512, split=True)
    kv = kv.reshape(n_kvcols // HEAD_DIM, batch, seq_len, HEAD_DIM)

    n_cmp = (seq_len - CMP_LEN) // CMP_STRIDE + 1
    n_slc = seq_len // SEL_LEN
    nb16 = seq_len // CMP_STRIDE
    half = CMP_STRIDE * HEAD_DIM
    kv16 = kv[:2 * N_KV].reshape(2 * N_KV, batch, nb16, half)
    w1cat = jnp.concatenate([cmp_w1[:, :half], cmp_w1[:, half:]], axis=2).astype(BF16)
    pe2 = jnp.pad(cmp_pe.reshape(2, 2, half), ((0, 0), (0, 2 * SUBLANES - 2), (0, 0)))
    kcv = _compress(kv16, w1cat, pe2, cmp_b1[:, None], cmp_w2.astype(BF16), cmp_b2[:, None], batch, n_cmp)

    tq = 512
    nsp = FLAG_COL - HEAD_DIM
    assert n_slc <= nsp and WINDOW % tq == 0 and seq_len % tq == 0
    onehot = (jnp.arange(seq_len)[:, None] // SEL_LEN == jnp.arange(HEAD_DIM)[None, :]).astype(BF16)
    kaug = jnp.concatenate(
        [kv[2 * N_KV:3 * N_KV], jnp.broadcast_to(onehot, (N_KV, batch, seq_len, HEAD_DIM))], axis=-1)
    assert seq_len % SEL_KEY_TILE == 0
    vst = _key_tiles_t(kv[3 * N_KV:4 * N_KV], SEL_KEY_TILE)
    padflag = (jnp.arange(seq_len + WINDOW)[:, None] < WINDOW) & (jnp.arange(HEAD_DIM)[None, :] == FLAG_COL - HEAD_DIM)
    front = ((0, 0), (0, 0), (WINDOW, 0), (0, 0))
    kwa = jnp.concatenate(
        [jnp.pad(kv[4 * N_KV:5 * N_KV], front),
         jnp.broadcast_to(padflag.astype(BF16), (N_KV, batch, seq_len + WINDOW, HEAD_DIM))], axis=-1)
    vwt = _key_tiles_t(jnp.pad(kv[5 * N_KV:6 * N_KV], front), tq)
    vct = jnp.swapaxes(kcv[1], -1, -2)
    ovt = _selection_overlap_t(n_cmp, nb16, n_slc, nsp)

    for layer in range(n_a, depth):
        j = layer - n_a
        shift, scale, gate = (mods[layer][:, None, i * d:(i + 1) * d] for i in range(3))
        wq, cs = _qg_weight(nsa_w_qg[j])
        proj = _normmod_matmul(x2d, norm_g[layer][None], shift, scale, wq, cs, seq_len, tm=PROJ_ROWS, tn=512)
        o = _nsa_attention_t(proj, kcv, vct, kaug, vst, kwa, vwt, ovt, batch, seq_len, tq=tq)
        x2d = _mm_residual(o, nsa_w_o[j].astype(BF16), x2d, gate, final_norm_g[None], seq_len, tm=512,
                           final_norm=(layer == depth - 1))

    return x2d.reshape(batch, seq_len, d)
```

```python
import functools
import math

import jax
import jax.numpy as jnp
import numpy as np
from jax import lax
from jax.experimental import pallas as pl
from jax.experimental.pallas import tpu as pltpu

F32 = jnp.float32
BF16 = jnp.bfloat16

SSM_GROUP = 16
SSM_STATE = 64
N_HEADS = 16
N_KV = 4
HPG = N_HEADS // N_KV
HEAD_DIM = 128
N_BRANCH = 3
CMP_LEN = 32
CMP_STRIDE = 16
SEL_LEN = 64
SEL_TOPK = 16
WINDOW = 512
SEL_BONUS = 1e3
NEG = -1e30
EPS = 1e-6

LANES = 128
SUBLANES = 8
MXU_DIM = 256
VMEM_LIMIT = 56 * 1024 * 1024

PROJ_ROWS = 1024

GROUPS_PER_BLOCK = MXU_DIM // SSM_GROUP
STATES_PER_BLOCK = GROUPS_PER_BLOCK * SSM_STATE


def _cparams(sem):
    return pltpu.CompilerParams(dimension_semantics=sem, vmem_limit_bytes=VMEM_LIMIT)


def _gelu_tanh(x):
    return x * (0.5 * (1.0 + jnp.tanh(math.sqrt(2.0 / math.pi) * (x + 0.044715 * (x * x * x)))))


def _sigmoid(x):
    return 0.5 * jnp.tanh(0.5 * x) + 0.5


def _dot_nt(a, b):
    return lax.dot_general(a, b, (((1,), (1,)), ((), ())), preferred_element_type=F32)


def _cond_kernel(c_ref, w_ref, b_ref, o_ref):
    c = c_ref[...]
    ca = (c * _sigmoid(c)).astype(BF16)
    acc = jnp.dot(ca, w_ref[0].astype(BF16), preferred_element_type=F32)
    o_ref[0] = acc + b_ref[0]


def _cond_matmul(c_pad, w, b, tn=512):
    nl, d, n = w.shape
    r = c_pad.shape[0]
    return pl.pallas_call(
        _cond_kernel,
        grid=(nl, n // tn),
        in_specs=[
            pl.BlockSpec((r, d), lambda l, j: (0, 0)),
            pl.BlockSpec((1, d, tn), lambda l, j: (l, 0, j)),
            pl.BlockSpec((1, 1, tn), lambda l, j: (l, 0, j)),
        ],
        out_specs=pl.BlockSpec((1, r, tn), lambda l, j: (l, 0, j)),
        out_shape=jax.ShapeDtypeStruct((nl, r, n), F32),
        compiler_params=_cparams(("parallel", "parallel")),
        name="cond_matmul",
    )(c_pad, w, b)


def _normmod_mm_kernel(x_ref, g_ref, sh_ref, sc_ref, w_ref, cs_ref, o_ref, h_ref):
    @pl.when(pl.program_id(1) == 0)
    def _():
        x = x_ref[...]
        y = x * lax.rsqrt(jnp.mean(x * x, axis=-1, keepdims=True) + EPS)
        y = y * g_ref[...]
        h_ref[...] = (y * (1.0 + sc_ref[0]) + sh_ref[0]).astype(BF16)

    acc = jnp.dot(h_ref[...], w_ref[...], preferred_element_type=F32) * cs_ref[...]
    o_ref[...] = acc.astype(o_ref.dtype)


def _normmod_matmul(x2d, g, shift, scale, w, colscale, seq_len, *, tm, tn):
    m, d = x2d.shape
    n = w.shape[1]
    rows_per_batch = seq_len // tm
    return pl.pallas_call(
        _normmod_mm_kernel,
        grid=(m // tm, n // tn),
        in_specs=[
            pl.BlockSpec((tm, d), lambda i, j: (i, 0)),
            pl.BlockSpec((1, d), lambda i, j: (0, 0)),
            pl.BlockSpec((1, 1, d), lambda i, j: (i // rows_per_batch, 0, 0)),
            pl.BlockSpec((1, 1, d), lambda i, j: (i // rows_per_batch, 0, 0)),
            pl.BlockSpec((d, tn), lambda i, j: (0, j)),
            pl.BlockSpec((1, tn), lambda i, j: (0, j)),
        ],
        out_specs=pl.BlockSpec((tm, tn), lambda i, j: (i, j)),
        out_shape=jax.ShapeDtypeStruct((m, n), BF16),
        scratch_shapes=[pltpu.VMEM((tm, d), BF16)],
        compiler_params=_cparams(("parallel", "arbitrary")),
        name="normmod_matmul",
    )(x2d, g, shift, scale, w, colscale)


KV_KCMP, KV_VCMP, KV_KSEL, KV_KWIN = range(4)
VT_SEL, VT_WIN = range(2)
_SEL_V_COL, _WIN_V_COL = 3, 5


def _kv_proj_kernel(x_ref, g_ref, sh_ref, sc_ref, w_ref, ks_ref, vt_ref, h_ref):
    j = pl.program_id(1)

    @pl.when(j == 0)
    def _():
        x = x_ref[...]
        y = x * lax.rsqrt(jnp.mean(x * x, axis=-1, keepdims=True) + EPS)
        y = y * g_ref[...]
        h_ref[...] = (y * (1.0 + sc_ref[0]) + sh_ref[0]).astype(BF16)

    acc = jnp.dot(h_ref[...], w_ref[...], preferred_element_type=F32)
    tk = vt_ref.shape[-1]
    is_value = (j == _SEL_V_COL) | (j == _WIN_V_COL)

    @pl.when(jnp.logical_not(is_value))
    def _():
        for s in range(N_KV):
            ks_ref[s] = acc[:, s * HEAD_DIM:(s + 1) * HEAD_DIM].astype(ks_ref.dtype)

    @pl.when(is_value)
    def _():
        for s in range(N_KV):
            for r in range(acc.shape[0] // tk):
                blk = acc[r * tk:(r + 1) * tk, s * HEAD_DIM:(s + 1) * HEAD_DIM]
                vt_ref[s, 0, r] = blk.T.astype(vt_ref.dtype)


def _kv_projection(x2d, g, shift, scale, w, batch, seq_len, *, tm, tk):
    m, d = x2d.shape
    tn = N_KV * HEAD_DIM
    assert w.shape[1] == 6 * tn
    rpb = seq_len // tm

    def slab_kind(j):
        return j - (j >= _SEL_V_COL).astype(jnp.int32) - (j >= _WIN_V_COL).astype(jnp.int32)

    return pl.pallas_call(
        _kv_proj_kernel,
        grid=(m // tm, 6),
        in_specs=[
            pl.BlockSpec((tm, d), lambda i, j: (i, 0)),
            pl.BlockSpec((1, d), lambda i, j: (0, 0)),
            pl.BlockSpec((1, 1, d), lambda i, j: (i // rpb, 0, 0)),
            pl.BlockSpec((1, 1, d), lambda i, j: (i // rpb, 0, 0)),
            pl.BlockSpec((d, tn), lambda i, j: (0, j)),
        ],
        out_specs=[
            pl.BlockSpec((N_KV, tm, HEAD_DIM), lambda i, j: (slab_kind(j), i, 0)),
            pl.BlockSpec((N_KV, 1, tm // tk, HEAD_DIM, tk),
                         lambda i, j: ((j > _SEL_V_COL).astype(jnp.int32), i // rpb, i % rpb, 0, 0)),
        ],
        out_shape=[
            jax.ShapeDtypeStruct((4 * N_KV, m, HEAD_DIM), BF16),
            jax.ShapeDtypeStruct((2 * N_KV, batch, seq_len // tk, HEAD_DIM, tk), BF16),
        ],
        scratch_shapes=[pltpu.VMEM((tm, d), BF16)],
        compiler_params=_cparams(("parallel", "arbitrary")),
        name="kv_projection",
    )(x2d, g, shift, scale, w)


def _mm_res_kernel(a_ref, w_ref, x_ref, gate_ref, fg_ref, o_ref, *, final_norm):
    acc = jnp.dot(a_ref[...], w_ref[...], preferred_element_type=F32)
    xn = x_ref[...] + gate_ref[0] * acc
    if final_norm:
        xn = xn * lax.rsqrt(jnp.mean(xn * xn, axis=-1, keepdims=True) + EPS) * fg_ref[...]
    o_ref[...] = xn


def _mm_residual(a, w, x2d, gate, final_g, seq_len, *, tm, final_norm):
    m, k = a.shape
    n = w.shape[1]
    rows_per_batch = seq_len // tm
    return pl.pallas_call(
        functools.partial(_mm_res_kernel, final_norm=final_norm),
        grid=(m // tm,),
        in_specs=[
            pl.BlockSpec((tm, k), lambda i: (i, 0)),
            pl.BlockSpec((k, n), lambda i: (0, 0), pipeline_mode=pl.Buffered(1)),
            pl.BlockSpec((tm, n), lambda i: (i, 0)),
            pl.BlockSpec((1, 1, n), lambda i: (i // rows_per_batch, 0, 0)),
            pl.BlockSpec((1, n), lambda i: (0, 0)),
        ],
        out_specs=pl.BlockSpec((tm, n), lambda i: (i, 0)),
        out_shape=jax.ShapeDtypeStruct((m, n), F32),
        compiler_params=_cparams(("parallel",)),
        name="matmul_residual_final" if final_norm else "matmul_residual",
    )(a, w, x2d, gate, final_g)


def _glu_out_kernel(y_ref, z_ref, wg_ref, bg_ref, wo_ref, x_ref, gate_ref, fg_ref, o_ref, *, final_norm):
    y = y_ref[...]
    lin = jnp.dot(y, wg_ref[...], preferred_element_type=F32) + bg_ref[...]
    z = z_ref[...].astype(F32)
    t = ((y.astype(F32) * _sigmoid(lin)) * (z * _sigmoid(z))).astype(BF16)
    xn = x_ref[...] + gate_ref[0] * jnp.dot(t, wo_ref[...], preferred_element_type=F32)
    if final_norm:
        xn = xn * lax.rsqrt(jnp.mean(xn * xn, axis=-1, keepdims=True) + EPS) * fg_ref[...]
    o_ref[...] = xn


def _glu_out_residual(y, uz, w_glu, b_glu, w_out, x2d, gate, final_g, seq_len, *, tm, final_norm):
    m, e = y.shape
    n = w_out.shape[1]
    rows_per_batch = seq_len // tm
    resident = pl.Buffered(1)
    return pl.pallas_call(
        functools.partial(_glu_out_kernel, final_norm=final_norm),
        grid=(m // tm,),
        in_specs=[
            pl.BlockSpec((tm, e), lambda i: (i, 0)),
            pl.BlockSpec((tm, e), lambda i: (i, 1)),
            pl.BlockSpec((e, e), lambda i: (0, 0), pipeline_mode=resident),
            pl.BlockSpec((1, e), lambda i: (0, 0)),
            pl.BlockSpec((e, n), lambda i: (0, 0), pipeline_mode=resident),
            pl.BlockSpec((tm, n), lambda i: (i, 0)),
            pl.BlockSpec((1, 1, n), lambda i: (i // rows_per_batch, 0, 0)),
            pl.BlockSpec((1, n), lambda i: (0, 0)),
        ],
        out_specs=pl.BlockSpec((tm, n), lambda i: (i, 0)),
        out_shape=jax.ShapeDtypeStruct((m, n), F32),
        compiler_params=_cparams(("parallel",)),
        name="s5_glu_out",
    )(y, uz, w_glu, b_glu, w_out, x2d, gate, final_g)


def _s5_scan_kernel(u_ref, wb_ref, wc_ref, are_ref, aim_ref, d_ref, y_ref, s_ref, st_ref, *, steps, nblk):
    nslab = STATES_PER_BLOCK // LANES

    @pl.when(pl.program_id(1) == 0)
    def _():
        st_ref[...] = jnp.zeros_like(st_ref)

    for blk in range(nblk):
        bu = jnp.dot(u_ref[:, blk * MXU_DIM:(blk + 1) * MXU_DIM], wb_ref[blk], preferred_element_type=F32)
        for k in range(2 * nslab):
            s_ref[k, pl.ds(blk, steps, stride=nblk), :] = bu[:, k * LANES:(k + 1) * LANES]

    a_re = [are_ref[:, k * LANES:(k + 1) * LANES] for k in range(nslab)]
    a_im = [aim_ref[:, k * LANES:(k + 1) * LANES] for k in range(nslab)]

    def step(t, carry):
        xr, xi = carry
        r0 = pl.multiple_of(t * nblk, nblk)
        nr, ni = [], []
        for k in range(nslab):
            b_re = s_ref[k, pl.ds(r0, nblk), :]
            b_im = s_ref[nslab + k, pl.ds(r0, nblk), :]
            v_re = a_re[k] * xr[k] - a_im[k] * xi[k] + b_re
            v_im = a_re[k] * xi[k] + a_im[k] * xr[k] + b_im
            s_ref[k, pl.ds(r0, nblk), :] = v_re
            s_ref[nslab + k, pl.ds(r0, nblk), :] = v_im
            nr.append(v_re)
            ni.append(v_im)
        return tuple(nr), tuple(ni)

    init = (tuple(st_ref[k] for k in range(nslab)), tuple(st_ref[nslab + k] for k in range(nslab)))
    xr, xi = lax.fori_loop(0, steps, step, init, unroll=2)
    for k in range(nslab):
        st_ref[k] = xr[k]
        st_ref[nslab + k] = xi[k]

    for blk in range(nblk):
        xs = jnp.concatenate(
            [s_ref[k, pl.ds(blk, steps, stride=nblk), :].astype(BF16) for k in range(2 * nslab)], axis=1)
        cols = slice(blk * MXU_DIM, (blk + 1) * MXU_DIM)
        yb = jnp.dot(xs, wc_ref[blk], preferred_element_type=F32)
        yb = yb + d_ref[:, cols] * u_ref[:, cols].astype(F32)
        y_ref[:, cols] = _gelu_tanh(yb).astype(y_ref.dtype)


def _s5_scan(uz, wb, wc, a_re, a_im, d_skip, batch, seq_len, *, steps):
    m = uz.shape[0]
    nblk, _, two_states = wb.shape
    e = nblk * MXU_DIM
    chunks = seq_len // steps
    nslab2 = two_states // LANES
    return pl.pallas_call(
        functools.partial(_s5_scan_kernel, steps=steps, nblk=nblk),
        grid=(batch, chunks),
        in_specs=[
            pl.BlockSpec((steps, e), lambda b, c: (b * chunks + c, 0)),
            pl.BlockSpec(wb.shape, lambda b, c: (0, 0, 0), pipeline_mode=pl.Buffered(1)),
            pl.BlockSpec(wc.shape, lambda b, c: (0, 0, 0), pipeline_mode=pl.Buffered(1)),
            pl.BlockSpec(a_re.shape, lambda b, c: (0, 0)),
            pl.BlockSpec(a_im.shape, lambda b, c: (0, 0)),
            pl.BlockSpec((1, e), lambda b, c: (0, 0)),
        ],
        out_specs=pl.BlockSpec((steps, e), lambda b, c: (b * chunks + c, 0)),
        out_shape=jax.ShapeDtypeStruct((m, e), BF16),
        scratch_shapes=[
            pltpu.VMEM((nslab2, steps * nblk, LANES), F32),
            pltpu.VMEM((nslab2, nblk, LANES), F32),
        ],
        compiler_params=_cparams(("parallel", "arbitrary")),
        name="s5_scan",
    )(uz, wb, wc, a_re, a_im, d_skip)


def _s5_params(lam_re, lam_im, log_step, b_re, b_im, c_re, c_im):
    g, n = lam_re.shape
    dt = jnp.exp(log_step.astype(F32))[:, None]
    lr, li = lam_re.astype(F32), lam_im.astype(F32)
    mag = jnp.exp(lr * dt)
    a_re, a_im = mag * jnp.cos(li * dt), mag * jnp.sin(li * dt)
    den = lr * lr + li * li
    coef_re = ((a_re - 1.0) * lr + a_im * li) / den
    coef_im = (a_im * lr - (a_re - 1.0) * li) / den
    br, bi = b_re.astype(F32), b_im.astype(F32)
    bb_re = coef_re[..., None] * br - coef_im[..., None] * bi
    bb_im = coef_re[..., None] * bi + coef_im[..., None] * br
    nblk = g // GROUPS_PER_BLOCK
    eye = jnp.eye(GROUPS_PER_BLOCK, dtype=F32)

    def blockdiag_in(bb):
        t = bb.reshape(nblk, GROUPS_PER_BLOCK, n, SSM_GROUP)
        t = jnp.einsum('bgnc,gh->bgchn', t, eye)
        return t.reshape(nblk, GROUPS_PER_BLOCK * SSM_GROUP, GROUPS_PER_BLOCK * n)

    def blockdiag_out(cc):
        t = cc.astype(F32).reshape(nblk, GROUPS_PER_BLOCK, SSM_GROUP, n)
        t = jnp.einsum('bgcn,gh->bgnhc', t, eye)
        return t.reshape(nblk, GROUPS_PER_BLOCK * n, GROUPS_PER_BLOCK * SSM_GROUP)

    wb = jnp.concatenate([blockdiag_in(bb_re), blockdiag_in(bb_im)], axis=2).astype(BF16)
    wc = jnp.concatenate([blockdiag_out(c_re), -blockdiag_out(c_im)], axis=1).astype(BF16)
    return wb, wc, a_re.reshape(nblk, GROUPS_PER_BLOCK * n), a_im.reshape(nblk, GROUPS_PER_BLOCK * n)


def _compress_kernel(x_ref, w1_ref, pe_ref, b1_ref, w2_ref, b2_ref, o_ref, sh_ref, *, n_cmp):
    nb = x_ref.shape[2]
    w1 = w1_ref[0]
    r = jnp.dot(x_ref[0, 0], w1, preferred_element_type=F32)
    pe = pe_ref[0]
    pe_hi = pe.astype(BF16)
    pe_lo = (pe - pe_hi.astype(F32)).astype(BF16)
    pw = (jnp.dot(pe_hi, w1, preferred_element_type=F32) + jnp.dot(pe_lo, w1, preferred_element_type=F32))
    const = pw[0:1, :HEAD_DIM] + pw[1:2, HEAD_DIM:] + b1_ref[0]
    sh_ref[pl.ds(0, nb), :] = r[:, HEAD_DIM:]
    sh_ref[pl.ds(nb, SUBLANES), :] = jnp.zeros((SUBLANES, HEAD_DIM), F32)
    hid = r[:, :HEAD_DIM] + sh_ref[pl.ds(1, nb), :] + const
    out = jnp.dot(_gelu_tanh(hid).astype(BF16), w2_ref[0], preferred_element_type=F32) + b2_ref[0]
    row = lax.broadcasted_iota(jnp.int32, out.shape, 0)
    o_ref[0, 0, 0] = jnp.where(row < n_cmp, out, 0.0).astype(o_ref.dtype)


def _compress(kv16, w1cat, pe2, b1, w2, b2, batch, n_cmp):
    nb, wide = kv16.shape[2], kv16.shape[3]
    return pl.pallas_call(
        functools.partial(_compress_kernel, n_cmp=n_cmp),
        grid=(2, N_KV, batch),
        in_specs=[
            pl.BlockSpec((1, 1, nb, wide), lambda s, g, b: (s * N_KV + g, b, 0, 0)),
            pl.BlockSpec((1, wide, 2 * HEAD_DIM), lambda s, g, b: (s, 0, 0)),
            pl.BlockSpec((1, 2 * SUBLANES, wide), lambda s, g, b: (s, 0, 0)),
            pl.BlockSpec((1, 1, HEAD_DIM), lambda s, g, b: (s, 0, 0)),
            pl.BlockSpec((1, HEAD_DIM, HEAD_DIM), lambda s, g, b: (s, 0, 0)),
            pl.BlockSpec((1, 1, HEAD_DIM), lambda s, g, b: (s, 0, 0)),
        ],
        out_specs=pl.BlockSpec((1, 1, 1, nb, HEAD_DIM), lambda s, g, b: (s, b, g, 0, 0)),
        out_shape=jax.ShapeDtypeStruct((2, batch, N_KV, nb, HEAD_DIM), BF16),
        scratch_shapes=[pltpu.VMEM((nb + SUBLANES, HEAD_DIM), F32)],
        compiler_params=_cparams(("parallel", "parallel", "parallel")),
        name="nsa_compress",
    )(kv16, w1cat, pe2, b1, w2, b2)


FLAG_COL =HEAD_DIM + HEAD_DIM // 2
V_ROWS = HEAD_DIM + 16
WIN_QTILE = 512
SEL_KEY_TILE = 512


def _nsa_t_kernel(q_ref, z0_ref, z1_ref, z2_ref, gt_ref, kc_ref, vct_ref, ks_ref, vst_ref, kw_ref, vwt_ref,
                  ovt_ref, o_ref, ka_ref, kwa_ref, qa_ref, qw_ref, m_ref, acc_ref, ot_ref, s_ref, gz_ref,
                  *, tq, n_cmp, n_slc, topk):
    qi = pl.program_id(2)
    t0 = qi * tq
    rows = HPG * tq
    nsp = ovt_ref.shape[0]
    seq_len, aug = ka_ref.shape

    @pl.when(qi == 0)
    def _():
        key_blk = lax.broadcasted_iota(jnp.int32, (seq_len, aug - HEAD_DIM), 0) // SEL_LEN
        col = lax.broadcasted_iota(jnp.int32, (seq_len, aug - HEAD_DIM), 1)
        ka_ref[:, 0:HEAD_DIM] = ks_ref[0, 0]
        ka_ref[:, HEAD_DIM:] = jnp.where(key_blk == col, 1.0, 0.0).astype(BF16)
        pcol = lax.broadcasted_iota(jnp.int32, (WINDOW, aug), 1)
        kwa_ref[0:WINDOW, :] = jnp.where(pcol == FLAG_COL, 1.0, 0.0).astype(BF16)
        kwa_ref[WINDOW:, 0:HEAD_DIM] = kw_ref[0, 0]
        kwa_ref[WINDOW:, HEAD_DIM:] = jnp.zeros((seq_len, aug - HEAD_DIM), BF16)

    ones_rows = jnp.where(lax.broadcasted_iota(jnp.int32, (V_ROWS - HEAD_DIM, vst_ref.shape[4]), 0) == 0,
                          1.0, 0.0).astype(BF16)

    def v_tile(vt_ref, idx):
        return jnp.concatenate([vt_ref[0, 0, idx], ones_rows], axis=0)

    q = q_ref[...]
    qh = jnp.concatenate([q[:, h * HEAD_DIM:(h + 1) * HEAD_DIM] for h in range(HPG)], axis=0)
    flag = jnp.where(lax.broadcasted_iota(jnp.int32, (tq, aug - FLAG_COL), 1) == 0, NEG, 0.0).astype(BF16)
    qa_ref[:, 0:HEAD_DIM] = qh
    qw_ref[:, 0:HEAD_DIM] = qh
    qw_ref[:, HEAD_DIM:FLAG_COL] = jnp.zeros((rows, FLAG_COL - HEAD_DIM), BF16)
    for h in range(HPG):
        qa_ref[h * tq:(h + 1) * tq, FLAG_COL:] = flag
        qw_ref[h * tq:(h + 1) * tq, FLAG_COL:] = flag

    wq = vwt_ref.shape[4]
    wrows = HPG * wq
    krow = lax.broadcasted_iota(jnp.int32, (wq, wrows), 0)
    tcol = lax.broadcasted_iota(jnp.int32, (wq, wrows), 1) % wq
    n_win = WINDOW // wq
    for j in range(tq // wq):
        qsub = jnp.concatenate([qw_ref[h * tq + j * wq:h * tq + (j + 1) * wq, :] for h in range(HPG)], axis=0)
        sw = _dot_nt(kwa_ref[pl.ds(pl.multiple_of(t0 + j * wq, wq), WINDOW + wq), :], qsub)
        pieces = [sw[dd * wq:(dd + 1) * wq] for dd in range(n_win + 1)]
        pieces[0] = jnp.where(krow > tcol, pieces[0], NEG)
        pieces[n_win] = jnp.where(krow <= tcol, pieces[n_win], NEG)
        m_w = functools.reduce(jnp.maximum, [jnp.max(s, axis=0, keepdims=True) for s in pieces])
        o_w = jnp.zeros((V_ROWS, wrows), F32)
        for dd, s in enumerate(pieces):
            vidx = jnp.maximum(qi * (tq // wq) + j + dd - n_win, 0)
            o_w = o_w + jnp.dot(v_tile(vwt_ref, vidx), jnp.exp((s - m_w).astype(BF16)),
                                preferred_element_type=F32)
        o_w = o_w[:HEAD_DIM] * (1.0 / o_w[HEAD_DIM:HEAD_DIM + 1])
        for h in range(HPG):
            ot_ref[2, :, h * tq + j * wq:h * tq + (j + 1) * wq] = o_w[:, h * wq:(h + 1) * wq]

    gates = _sigmoid(gt_ref[...].astype(F32))
    for br, z_ref in enumerate((z0_ref, z1_ref, z2_ref)):
        for h in range(HPG):
            cs = slice(h * HEAD_DIM, (h + 1) * HEAD_DIM)
            z = z_ref[:, cs].astype(F32)
            c = br * HPG + h
            gz_ref[br, :, cs] = gates[:, c:c + 1] * (z * _sigmoid(z))

    ncp = kc_ref.shape[3]
    st = _dot_nt(kc_ref[0, 0, 0], qh)
    n_idx = lax.broadcasted_iota(jnp.int32, (ncp, rows), 0)
    tpos = t0 + lax.broadcasted_iota(jnp.int32, (ncp, rows), 1) % tq
    cmask = (n_idx * CMP_STRIDE + (CMP_LEN - 1) <= tpos) & (n_idx < n_cmp)
    st = jnp.where(cmask, st, NEG)
    e = jnp.where(cmask, jnp.exp(st - jnp.max(st, axis=0, keepdims=True)), 0.0)
    l = jnp.sum(e, axis=0, keepdims=True)
    p = e * (1.0 / jnp.where(l > 0.0, l, 1.0))
    ot_ref[0] = jnp.dot(vct_ref[0, 0], p.astype(BF16), preferred_element_type=F32)
    p_cmp = p[:, 0:tq]
    for h in range(1, HPG):
        p_cmp = p_cmp + p[:, h * tq:(h + 1) * tq]

    p_hi = p_cmp.astype(BF16)
    r1 = p_cmp - p_hi.astype(F32)
    p_mid = r1.astype(BF16)
    p_lo = (r1 - p_mid.astype(F32)).astype(BF16)
    ovt = ovt_ref[...]
    p_slc = (jnp.dot(ovt, p_hi, preferred_element_type=F32) + jnp.dot(ovt, p_mid, preferred_element_type=F32)
             + jnp.dot(ovt, p_lo, preferred_element_type=F32))
    blk = lax.broadcasted_iota(jnp.int32, (nsp, tq), 0)
    cur = (t0 + lax.broadcasted_iota(jnp.int32, (nsp, tq), 1)) // SEL_LEN
    valid = (blk <= cur) & (blk < n_slc)
    forced = (blk == 0) | (blk == cur) | (blk == cur - 1)
    score = jnp.where(valid, p_slc + jnp.where(forced, SEL_BONUS, 0.0), -SEL_BONUS)
    nv = nsp // SUBLANES
    sc = [score[k * SUBLANES:(k + 1) * SUBLANES] for k in range(nv)]
    sub = lax.broadcasted_iota(jnp.int32, (SUBLANES, tq), 0)
    rank = [jnp.zeros((SUBLANES, tq), jnp.int32) for _ in range(nv)]
    for i in range(n_slc):
        si = jnp.broadcast_to(score[i:i + 1, :], (SUBLANES, tq))
        for k in range(nv):
            if k * SUBLANES > i:
                beats = si >= sc[k]
            elif k * SUBLANES + SUBLANES - 1 <= i:
                beats = si > sc[k]
            else:
                beats = (si > sc[k]) | ((si == sc[k]) & (sub > i - k * SUBLANES))
            rank[k] = rank[k] + jnp.where(beats, 1, 0)
    rank = jnp.concatenate(rank, axis=0)
    sel = (rank < topk) & (score > -0.5 * SEL_BONUS)
    selb = jnp.where(sel, 0.0, NEG).T.astype(BF16)
    for h in range(HPG):
        qa_ref[h * tq:(h + 1) * tq, HEAD_DIM:FLAG_COL] = selb

    tk = s_ref.shape[0]

    def scores(k_tile):
        return _dot_nt(k_tile, qa_ref[...])

    def online_update(s, vt_tile):
        m_old = m_ref[...]
        m_new = jnp.maximum(m_old, jnp.max(s, axis=0, keepdims=True))
        alpha = jnp.exp(m_old - m_new)
        pt = jnp.exp((s - m_new).astype(BF16))
        acc_ref[...] = alpha * acc_ref[...] + jnp.dot(vt_tile, pt, preferred_element_type=F32)
        m_ref[...] = m_new

    m_ref[...] = jnp.full(m_ref.shape, NEG, F32)
    acc_ref[...] = jnp.zeros(acc_ref.shape, F32)
    n_full = t0 // tk
    s_ref[...] = scores(ka_ref[pl.ds(0, tk), :])

    def sel_body(kt, carry):
        k1 = pl.multiple_of((kt + 1) * tk, tk)
        s_next = scores(ka_ref[pl.ds(k1, tk), :])
        online_update(s_ref[...], v_tile(vst_ref, kt))
        s_ref[...] = s_next
        return carry

    lax.fori_loop(0, n_full, sel_body, 0)
    kpos = n_full * tk + lax.broadcasted_iota(jnp.int32, (tk, rows), 0)
    tpos_k = t0 + lax.broadcasted_iota(jnp.int32, (tk, rows), 1) % tq
    online_update(jnp.where(kpos <= tpos_k, s_ref[...], NEG), v_tile(vst_ref, n_full))
    ot_ref[1] = acc_ref[0:HEAD_DIM, :] * (1.0 / acc_ref[HEAD_DIM:HEAD_DIM + 1, :])

    for h in range(HPG):
        cs = slice(h * HEAD_DIM, (h + 1) * HEAD_DIM)
        tot = None
        for br in range(N_BRANCH):
            term = gz_ref[br, :, cs] * ot_ref[br, :, h * tq:(h + 1) * tq].T
            tot = term if tot is None else tot + term
        o_ref[:, cs] = tot.astype(o_ref.dtype)


def _nsa_attention_t(proj, kcv, vct, kslab, vt, ovt, batch, seq_len, *, tq):
    m = proj.shape[0]
    gw = HPG * HEAD_DIM
    qt = seq_len // tq
    n_cmp = (seq_len - CMP_LEN) // CMP_STRIDE + 1
    n_slc = seq_len // SEL_LEN
    z_base = N_KV
    gate_base = (N_KV + N_BRANCH * N_KV) * (gw // LANES)
    ncp = kcv.shape[3]
    aug = 2 * HEAD_DIM
    rows = HPG * tq
    tk = vt.shape[-1]

    def zspec(br):
        return pl.BlockSpec((tq, gw), lambda b, g, i: (b * qt + i, z_base + br * N_KV + g))

    def per_group(arr, kind):
        return pl.BlockSpec((1, 1) + arr.shape[2:], lambda b, g, i: (kind * N_KV + g, b) + (0,) * (arr.ndim - 2))

    return pl.pallas_call(
        functools.partial(_nsa_t_kernel, tq=tq, n_cmp=n_cmp, n_slc=n_slc, topk=min(SEL_TOPK, n_slc)),
        grid=(batch, N_KV, qt),
        in_specs=[
            pl.BlockSpec((tq, gw), lambda b, g, i: (b * qt + i, g)),
            zspec(0), zspec(1), zspec(2),
            pl.BlockSpec((tq, LANES), lambda b, g, i: (b * qt + i, gate_base + g)),
            pl.BlockSpec((1, 1, 1, ncp, HEAD_DIM), lambda b, g, i: (0, b, g, 0, 0)),
            pl.BlockSpec((1, 1, HEAD_DIM, ncp), lambda b, g, i: (b, g, 0, 0)),
            per_group(kslab, KV_KSEL), per_group(vt, VT_SEL), per_group(kslab, KV_KWIN), per_group(vt, VT_WIN),
            pl.BlockSpec(ovt.shape, lambda b, g, i: (0, 0)),
        ],
        out_specs=pl.BlockSpec((tq, gw), lambda b, g, i: (b * qt + i, g)),
        out_shape=jax.ShapeDtypeStruct((m, N_KV * gw), BF16),
        scratch_shapes=[
            pltpu.VMEM((seq_len, aug), BF16),
            pltpu.VMEM((seq_len + WINDOW, aug), BF16),
            pltpu.VMEM((rows, aug), BF16),
            pltpu.VMEM((rows, aug), BF16),
            pltpu.VMEM((1, rows), F32),
            pltpu.VMEM((V_ROWS, rows), F32),
            pltpu.VMEM((N_BRANCH, HEAD_DIM, rows), F32),
            pltpu.VMEM((tk, rows), F32),
            pltpu.VMEM((N_BRANCH, tq, gw), F32),
        ],
        compiler_params=_cparams(("parallel", "parallel", "arbitrary")),
        name="nsa_attention",
    )(proj, proj, proj, proj, proj, kcv, vct, kslab, vt, kslab, vt, ovt)


def _selection_overlap_t(n_cmp, ncp, n_slc, nsp):
    c0 = np.arange(ncp)[None, :] * CMP_STRIDE
    s0 = np.arange(nsp)[:, None] * SEL_LEN
    ov = np.clip(np.minimum(c0 + CMP_LEN, s0 + SEL_LEN) - np.maximum(c0, s0), 0, None) / CMP_STRIDE
    ov = ov * (np.arange(ncp)[None, :] < n_cmp) * (np.arange(nsp)[:, None] < n_slc)
    return jnp.asarray(ov, dtype=BF16)


def _s5_layer(x2d, mods, norm_g, w_in, s5p, d_skip, w_glu, b_glu, w_out, batch, seq_len, final_g, final_norm):
    d = x2d.shape[1]
    shift, scale, gate = (mods[:, None, i * d:(i + 1) * d] for i in range(3))
    e = w_glu.shape[0]
    uz = _normmod_matmul(x2d, norm_g[None], shift, scale, w_in.astype(BF16), jnp.ones((1, 2 * e), F32),
                         seq_len, tm=PROJ_ROWS, tn=1024)
    wb, wc, a_re, a_im = s5p
    y = _s5_scan(uz, wb, wc, a_re, a_im, d_skip[None].astype(F32), batch, seq_len, steps=256)
    return _glu_out_residual(y, uz, w_glu.astype(BF16), b_glu[None].astype(F32), w_out.astype(BF16), x2d, gate,
                             final_g[None], seq_len, tm=512, final_norm=final_norm)


def _qg_weight(w_qg):
    d = w_qg.shape[0]
    att = N_HEADS * HEAD_DIM
    g_end = att + N_BRANCH * N_HEADS
    wg = w_qg[:, att:g_end].reshape(d, N_BRANCH, N_KV, HPG)
    wg = jnp.transpose(wg, (0, 2, 1, 3)).reshape(d, N_KV, N_BRANCH * HPG)
    wg = jnp.pad(wg, ((0, 0), (0, 0), (0, LANES - N_BRANCH * HPG))).reshape(d, N_KV * LANES)
    w = jnp.concatenate([w_qg[:, :att], w_qg[:, g_end:], wg], axis=1).astype(BF16)
    cs = jnp.concatenate([jnp.full((att,), HEAD_DIM ** -0.5, F32), jnp.ones((w.shape[1] - att,), F32)])
    return w, cs[None]


def kernel(x, c, norm_g, mod_w, mod_b, ssm_w_in, ssm_lam_re, ssm_lam_im, ssm_log_step, ssm_b_re, ssm_b_im, ssm_c_re, ssm_c_im, ssm_d, ssm_w_glu, ssm_b_glu, ssm_w_out, kv_norm_g, kv_mod_w, kv_mod_b, w_kv, cmp_pe, cmp_w1, cmp_b1, cmp_w2, cmp_b2, nsa_w_qg, nsa_w_o, final_norm_g):
    batch, seq_len, d = x.shape
    depth = mod_w.shape[0]
    n_a = ssm_w_in.shape[0]
    m = batch * seq_len
    x2d = x.reshape(m, d)

    c_pad = jnp.pad(c, ((0, 2 * SUBLANES - batch), (0, 0)))
    mods = _cond_matmul(c_pad, mod_w, mod_b[:, None])[:, :batch]
    kv_mods = _cond_matmul(c_pad, kv_mod_w[None], kv_mod_b[None, None])[0, :batch]

    for layer in range(n_a):
        s5p = _s5_params(ssm_lam_re[layer], ssm_lam_im[layer], ssm_log_step[layer], ssm_b_re[layer],
                         ssm_b_im[layer], ssm_c_re[layer], ssm_c_im[layer])
        x2d = _s5_layer(x2d, mods[layer], norm_g[layer], ssm_w_in[layer], s5p, ssm_d[layer], ssm_w_glu[layer],
                        ssm_b_glu[layer], ssm_w_out[layer], batch, seq_len, final_norm_g,
                        final_norm=(layer == depth - 1))

    tq = 512
    assert SEL_KEY_TILE == WIN_QTILE and seq_len % SEL_KEY_TILE == 0
    kv_shift, kv_scale = kv_mods[:, None, :d], kv_mods[:, None, d:]
    kslab, vt = _kv_projection(x2d, kv_norm_g[None], kv_shift, kv_scale, w_kv.astype(BF16), batch, seq_len,
                               tm=PROJ_ROWS, tk=SEL_KEY_TILE)
    kslab = kslab.reshape(4 * N_KV, batch, seq_len, HEAD_DIM)

    n_cmp = (seq_len - CMP_LEN) // CMP_STRIDE + 1
    n_slc = seq_len // SEL_LEN
    nb16 = seq_len // CMP_STRIDE
    half = CMP_STRIDE * HEAD_DIM
    kv16 = kslab[:2 * N_KV].reshape(2 * N_KV, batch, nb16, half)
    w1cat = jnp.concatenate([cmp_w1[:, :half], cmp_w1[:, half:]], axis=2).astype(BF16)
    pe2 = jnp.pad(cmp_pe.reshape(2, 2, half), ((0, 0), (0, 2 * SUBLANES - 2), (0, 0)))
    kcv = _compress(kv16, w1cat, pe2, cmp_b1[:, None], cmp_w2.astype(BF16), cmp_b2[:, None], batch, n_cmp)

    nsp = FLAG_COL - HEAD_DIM
    assert n_slc <= nsp and WINDOW % WIN_QTILE == 0 and tq % WIN_QTILE == 0 and seq_len % tq == 0
    vct = jnp.swapaxes(kcv[1], -1, -2)
    ovt = _selection_overlap_t(n_cmp, nb16, n_slc, nsp)

    for layer in range(n_a, depth):
        j = layer - n_a
        shift, scale, gate = (mods[layer][:, None, i * d:(i + 1) * d] for i in range(3))
        wq, cs = _qg_weight(nsa_w_qg[j])
        proj = _normmod_matmul(x2d, norm_g[layer][None], shift, scale, wq, cs, seq_len, tm=PROJ_ROWS, tn=512)
        o = _nsa_attention_t(proj, kcv, vct, kslab, vt, ovt, batch, seq_len, tq=tq)
        x2d = _mm_residual(o, nsa_w_o[j].astype(BF16), x2d, gate, final_norm_g[None], seq_len, tm=512,
                           final_norm=(layer == depth - 1))

    return x2d.reshape(batch, seq_len, d)
```

```python
import functools
import math

import jax
import jax.numpy as jnp
import numpy as np
from jax import lax
from jax.experimental import pallas as pl
from jax.experimental.pallas import tpu as pltpu

F32 = jnp.float32
BF16 = jnp.bfloat16

SSM_GROUP = 16
SSM_STATE = 64
N_HEADS = 16
N_KV = 4
HPG = N_HEADS // N_KV
HEAD_DIM = 128
N_BRANCH = 3
CMP_LEN = 32
CMP_STRIDE = 16
SEL_LEN = 64
SEL_TOPK = 16
WINDOW = 512
SEL_BONUS = 1e3
NEG = -1e30
EPS = 1e-6

LANES = 128
SUBLANES = 8
MXU_DIM = 256
VMEM_LIMIT = 56 * 1024 * 1024

PROJ_ROWS = 1024

GROUPS_PER_BLOCK = MXU_DIM // SSM_GROUP
STATES_PER_BLOCK = GROUPS_PER_BLOCK * SSM_STATE


def _cparams(sem):
    return pltpu.CompilerParams(dimension_semantics=sem, vmem_limit_bytes=VMEM_LIMIT)


def _gelu_tanh(x):
    return x * (0.5 * (1.0 + jnp.tanh(math.sqrt(2.0 / math.pi) * (x + 0.044715 * (x * x * x)))))


def _sigmoid(x):
    return 0.5 * jnp.tanh(0.5 * x) + 0.5


def _dot_nt(a, b):
    return lax.dot_general(a, b, (((1,), (1,)), ((), ())), preferred_element_type=F32)


def _cond_kernel(c_ref, w_ref, b_ref, o_ref):
    c = c_ref[...]
    ca = (c * _sigmoid(c)).astype(BF16)
    acc = jnp.dot(ca, w_ref[0].astype(BF16), preferred_element_type=F32)
    o_ref[0] = acc + b_ref[0]


def _cond_matmul(c_pad, w, b, tn=512):
    nl, d, n = w.shape
    r = c_pad.shape[0]
    return pl.pallas_call(
        _cond_kernel,
        grid=(nl, n // tn),
        in_specs=[
            pl.BlockSpec((r, d), lambda l, j: (0, 0)),
            pl.BlockSpec((1, d, tn), lambda l, j: (l, 0, j)),
            pl.BlockSpec((1, 1, tn), lambda l, j: (l, 0, j)),
        ],
        out_specs=pl.BlockSpec((1, r, tn), lambda l, j: (l, 0, j)),
        out_shape=jax.ShapeDtypeStruct((nl, r, n), F32),
        compiler_params=_cparams(("parallel", "parallel")),
        name="cond_matmul",
    )(c_pad, w, b)


NORM_SLAB = 16


def _norm_modulate(x_ref, g_ref, sh_ref, sc_ref, h_ref):
    gain = g_ref[...] * (1.0 + sc_ref[0])
    shift = sh_ref[0]

    def slab(r, carry):
        rs = pl.ds(pl.multiple_of(r * NORM_SLAB, NORM_SLAB), NORM_SLAB)
        x = x_ref[rs, :]
        y = x * lax.rsqrt(jnp.mean(x * x, axis=-1, keepdims=True) + EPS)
        h_ref[rs, :] = (y * gain + shift).astype(BF16)
        return carry

    lax.fori_loop(0, x_ref.shape[0] // NORM_SLAB, slab, 0, unroll=8)


def _normmod_mm_kernel(x_ref, g_ref, sh_ref, sc_ref, w_ref, cs_ref, o_ref, h_ref):
    @pl.when(pl.program_id(1) == 0)
    def _():
        _norm_modulate(x_ref, g_ref, sh_ref, sc_ref, h_ref)

    acc = jnp.dot(h_ref[...], w_ref[...], preferred_element_type=F32) * cs_ref[...]
    o_ref[...] = acc.astype(o_ref.dtype)


def _normmod_matmul(x2d, g, shift, scale, w, colscale, seq_len, *, tm, tn):
    m, d = x2d.shape
    n = w.shape[1]
    rows_per_batch = seq_len // tm
    return pl.pallas_call(
        _normmod_mm_kernel,
        grid=(m // tm, n // tn),
        in_specs=[
            pl.BlockSpec((tm, d), lambda i, j: (i, 0)),
            pl.BlockSpec((1, d), lambda i, j: (0, 0)),
            pl.BlockSpec((1, 1, d), lambda i, j: (i // rows_per_batch, 0, 0)),
            pl.BlockSpec((1, 1, d), lambda i, j: (i // rows_per_batch, 0, 0)),
            pl.BlockSpec((d, tn), lambda i, j: (0, j)),
            pl.BlockSpec((1, tn), lambda i, j: (0, j)),
        ],
        out_specs=pl.BlockSpec((tm, tn), lambda i, j: (i, j)),
        out_shape=jax.ShapeDtypeStruct((m, n), BF16),
        scratch_shapes=[pltpu.VMEM((tm, d), BF16)],
        compiler_params=_cparams(("parallel", "arbitrary")),
        name="normmod_matmul",
    )(x2d, g, shift, scale, w, colscale)


KV_KCMP, KV_VCMP, KV_KSEL, KV_KWIN = range(4)
VT_SEL, VT_WIN = range(2)
_SEL_V_COL, _WIN_V_COL = 3, 5


def _kv_proj_kernel(x_ref, g_ref, sh_ref, sc_ref, w_ref, ks_ref, vt_ref, h_ref):
    j = pl.program_id(1)

    @pl.when(j == 0)
    def _():
        _norm_modulate(x_ref, g_ref, sh_ref, sc_ref, h_ref)

    acc = jnp.dot(h_ref[...], w_ref[...], preferred_element_type=F32)
    tk = vt_ref.shape[-1]
    is_value = (j == _SEL_V_COL) | (j == _WIN_V_COL)

    @pl.when(jnp.logical_not(is_value))
    def _():
        for s in range(N_KV):
            ks_ref[s] = acc[:, s * HEAD_DIM:(s + 1) * HEAD_DIM].astype(ks_ref.dtype)

    @pl.when(is_value)
    def _():
        for s in range(N_KV):
            for r in range(acc.shape[0] // tk):
                blk = acc[r * tk:(r + 1) * tk, s * HEAD_DIM:(s + 1) * HEAD_DIM]
                vt_ref[s, 0, r] = blk.T.astype(vt_ref.dtype)


def _kv_projection(x2d, g, shift, scale, w, batch, seq_len, *, tm, tk):
    m, d = x2d.shape
    tn = N_KV * HEAD_DIM
    assert w.shape[1] == 6 * tn
    rpb = seq_len // tm

    def slab_kind(j):
        return j - (j >= _SEL_V_COL).astype(jnp.int32) - (j >= _WIN_V_COL).astype(jnp.int32)

    return pl.pallas_call(
        _kv_proj_kernel,
        grid=(m // tm, 6),
        in_specs=[
            pl.BlockSpec((tm, d), lambda i, j: (i, 0)),
            pl.BlockSpec((1, d), lambda i, j: (0, 0)),
            pl.BlockSpec((1, 1, d), lambda i, j: (i // rpb, 0, 0)),
            pl.BlockSpec((1, 1, d), lambda i, j: (i // rpb, 0, 0)),
            pl.BlockSpec((d, tn), lambda i, j: (0, j)),
        ],
        out_specs=[
            pl.BlockSpec((N_KV, tm, HEAD_DIM), lambda i, j: (slab_kind(j), i, 0)),
            pl.BlockSpec((N_KV, 1, tm // tk, HEAD_DIM, tk),
                         lambda i, j: ((j > _SEL_V_COL).astype(jnp.int32), i // rpb, i % rpb, 0, 0)),
        ],
        out_shape=[
            jax.ShapeDtypeStruct((4 * N_KV, m, HEAD_DIM), BF16),
            jax.ShapeDtypeStruct((2 * N_KV, batch, seq_len // tk, HEAD_DIM, tk), BF16),
        ],
        scratch_shapes=[pltpu.VMEM((tm, d), BF16)],
        compiler_params=_cparams(("parallel", "arbitrary")),
        name="kv_projection",
    )(x2d, g, shift, scale, w)


def _mm_res_kernel(a_ref, w_ref, x_ref, gate_ref, fg_ref, o_ref, *, final_norm):
    acc = jnp.dot(a_ref[...], w_ref[...], preferred_element_type=F32)
    xn = x_ref[...] + gate_ref[0] * acc
    if final_norm:
        xn = xn * lax.rsqrt(jnp.mean(xn * xn, axis=-1, keepdims=True) + EPS) * fg_ref[...]
    o_ref[...] = xn


def _mm_residual(a, w, x2d, gate, final_g, seq_len, *, tm, final_norm):
    m, k = a.shape
    n = w.shape[1]
    rows_per_batch = seq_len // tm
    return pl.pallas_call(
        functools.partial(_mm_res_kernel, final_norm=final_norm),
        grid=(m // tm,),
        in_specs=[
            pl.BlockSpec((tm, k), lambda i: (i, 0)),
            pl.BlockSpec((k, n), lambda i: (0, 0), pipeline_mode=pl.Buffered(1)),
            pl.BlockSpec((tm, n), lambda i: (i, 0)),
            pl.BlockSpec((1, 1, n), lambda i: (i // rows_per_batch, 0, 0)),
            pl.BlockSpec((1, n), lambda i: (0, 0)),
        ],
        out_specs=pl.BlockSpec((tm, n), lambda i: (i, 0)),
        out_shape=jax.ShapeDtypeStruct((m, n), F32),
        compiler_params=_cparams(("parallel",)),
        name="matmul_residual_final" if final_norm else "matmul_residual",
    )(a, w, x2d, gate, final_g)


def _glu_out_kernel(y_ref, z_ref, wg_ref, bg_ref, wo_ref, x_ref, gate_ref, fg_ref, o_ref, *, final_norm):
    y = y_ref[...]
    lin = jnp.dot(y, wg_ref[...], preferred_element_type=F32) + bg_ref[...]
    z = z_ref[...].astype(F32)
    t = ((y.astype(F32) * _sigmoid(lin)) * (z * _sigmoid(z))).astype(BF16)
    xn = x_ref[...] + gate_ref[0] * jnp.dot(t, wo_ref[...], preferred_element_type=F32)
    if final_norm:
        xn = xn * lax.rsqrt(jnp.mean(xn * xn, axis=-1, keepdims=True) + EPS) * fg_ref[...]
    o_ref[...] = xn


def _glu_out_residual(y, uz, w_glu, b_glu, w_out, x2d, gate, final_g, seq_len, *, tm, final_norm):
    m, e = y.shape
    n = w_out.shape[1]
    rows_per_batch = seq_len // tm
    resident = pl.Buffered(1)
    return pl.pallas_call(
        functools.partial(_glu_out_kernel, final_norm=final_norm),
        grid=(m // tm,),
        in_specs=[
            pl.BlockSpec((tm, e), lambda i: (i, 0)),
            pl.BlockSpec((tm, e), lambda i: (i, 1)),
            pl.BlockSpec((e, e), lambda i: (0, 0), pipeline_mode=resident),
            pl.BlockSpec((1, e), lambda i: (0, 0)),
            pl.BlockSpec((e, n), lambda i: (0, 0), pipeline_mode=resident),
            pl.BlockSpec((tm, n), lambda i: (i, 0)),
            pl.BlockSpec((1, 1, n), lambda i: (i // rows_per_batch, 0, 0)),
            pl.BlockSpec((1, n), lambda i: (0, 0)),
        ],
        out_specs=pl.BlockSpec((tm, n), lambda i: (i, 0)),
        out_shape=jax.ShapeDtypeStruct((m, n), F32),
        compiler_params=_cparams(("parallel",)),
        name="s5_glu_out",
    )(y, uz, w_glu, b_glu, w_out, x2d, gate, final_g)


def _s5_scan_kernel(u_ref, wb_ref, wc_ref, are_ref, aim_ref, d_ref, y_ref, s_ref, st_ref, *, steps, nblk):
    nslab = STATES_PER_BLOCK // LANES

    @pl.when(pl.program_id(1) == 0)
    def _():
        st_ref[...] = jnp.zeros_like(st_ref)

    for blk in range(nblk):
        bu = jnp.dot(u_ref[:, blk * MXU_DIM:(blk + 1) * MXU_DIM], wb_ref[blk], preferred_element_type=F32)
        for k in range(2 * nslab):
            s_ref[k, pl.ds(blk, steps, stride=nblk), :] = bu[:, k * LANES:(k + 1) * LANES]

    a_re = [are_ref[:, k * LANES:(k + 1) * LANES] for k in range(nslab)]
    a_im = [aim_ref[:, k * LANES:(k + 1) * LANES] for k in range(nslab)]

    def step(t, carry):
        xr, xi = carry
        r0 = pl.multiple_of(t * nblk, nblk)
        nr, ni = [], []
        for k in range(nslab):
            b_re = s_ref[k, pl.ds(r0, nblk), :]
            b_im = s_ref[nslab + k, pl.ds(r0, nblk), :]
            v_re = a_re[k] * xr[k] - a_im[k] * xi[k] + b_re
            v_im = a_re[k] * xi[k] + a_im[k] * xr[k] + b_im
            s_ref[k, pl.ds(r0, nblk), :] = v_re
            s_ref[nslab + k, pl.ds(r0, nblk), :] = v_im
            nr.append(v_re)
            ni.append(v_im)
        return tuple(nr), tuple(ni)

    init = (tuple(st_ref[k] for k in range(nslab)), tuple(st_ref[nslab + k] for k in range(nslab)))
    xr, xi = lax.fori_loop(0, steps, step, init, unroll=2)
    for k in range(nslab):
        st_ref[k] = xr[k]
        st_ref[nslab + k] = xi[k]

    for blk in range(nblk):
        xs = jnp.concatenate(
            [s_ref[k, pl.ds(blk, steps, stride=nblk), :].astype(BF16) for k in range(2 * nslab)], axis=1)
        cols = slice(blk * MXU_DIM, (blk + 1) * MXU_DIM)
        yb = jnp.dot(xs, wc_ref[blk], preferred_element_type=F32)
        yb = yb + d_ref[:, cols] * u_ref[:, cols].astype(F32)
        y_ref[:, cols] = _gelu_tanh(yb).astype(y_ref.dtype)


def _s5_scan(uz, wb, wc, a_re, a_im, d_skip, batch, seq_len, *, steps):
    m = uz.shape[0]
    nblk, _, two_states = wb.shape
    e = nblk * MXU_DIM
    chunks = seq_len // steps
    nslab2 = two_states // LANES
    return pl.pallas_call(
        functools.partial(_s5_scan_kernel, steps=steps, nblk=nblk),
        grid=(batch, chunks),
        in_specs=[
            pl.BlockSpec((steps, e), lambda b, c: (b * chunks + c, 0)),
            pl.BlockSpec(wb.shape, lambda b, c: (0, 0, 0), pipeline_mode=pl.Buffered(1)),
            pl.BlockSpec(wc.shape, lambda b, c: (0, 0, 0), pipeline_mode=pl.Buffered(1)),
            pl.BlockSpec(a_re.shape, lambda b, c: (0, 0)),
            pl.BlockSpec(a_im.shape, lambda b, c: (0, 0)),
            pl.BlockSpec((1, e), lambda b, c: (0, 0)),
        ],
        out_specs=pl.BlockSpec((steps, e), lambda b, c: (b * chunks + c, 0)),
        out_shape=jax.ShapeDtypeStruct((m, e), BF16),
        scratch_shapes=[
            pltpu.VMEM((nslab2, steps * nblk, LANES), F32),
            pltpu.VMEM((nslab2, nblk, LANES), F32),
        ],
        compiler_params=_cparams(("parallel", "arbitrary")),
        name="s5_scan",
    )(uz, wb, wc, a_re, a_im, d_skip)


def _s5_params(lam_re, lam_im, log_step, b_re, b_im, c_re, c_im):
    g, n = lam_re.shape
    dt = jnp.exp(log_step.astype(F32))[:, None]
    lr, li = lam_re.astype(F32), lam_im.astype(F32)
    mag = jnp.exp(lr * dt)
    a_re, a_im = mag * jnp.cos(li * dt), mag * jnp.sin(li * dt)
    den = lr * lr + li * li
    coef_re = ((a_re - 1.0) * lr + a_im * li) / den
    coef_im = (a_im * lr - (a_re - 1.0) * li) / den
    br, bi = b_re.astype(F32), b_im.astype(F32)
    bb_re = coef_re[..., None] * br - coef_im[..., None] * bi
    bb_im = coef_re[..., None] * bi + coef_im[..., None] * br
    nblk = g // GROUPS_PER_BLOCK
    eye = jnp.eye(GROUPS_PER_BLOCK, dtype=F32)

    def blockdiag_in(bb):
        t = bb.reshape(nblk, GROUPS_PER_BLOCK, n, SSM_GROUP)
        t = jnp.einsum('bgnc,gh->bgchn', t, eye)
        return t.reshape(nblk, GROUPS_PER_BLOCK * SSM_GROUP, GROUPS_PER_BLOCK * n)

    def blockdiag_out(cc):
        t = cc.astype(F32).reshape(nblk, GROUPS_PER_BLOCK, SSM_GROUP, n)
        t = jnp.einsum('bgcn,gh->bgnhc', t, eye)
        return t.reshape(nblk, GROUPS_PER_BLOCK * n, GROUPS_PER_BLOCK * SSM_GROUP)

    wb = jnp.concatenate([blockdiag_in(bb_re), blockdiag_in(bb_im)], axis=2).astype(BF16)
    wc = jnp.concatenate([blockdiag_out(c_re), -blockdiag_out(c_im)], axis=1).astype(BF16)
    return wb, wc, a_re.reshape(nblk, GROUPS_PER_BLOCK * n), a_im.reshape(nblk, GROUPS_PER_BLOCK * n)


def _compress_kernel(x_ref, w1_ref, pe_ref, b1_ref, w2_ref, b2_ref, o_ref, sh_ref, *, n_cmp):
    nb = x_ref.shape[2]
    w1 = w1_ref[0]
    r = jnp.dot(x_ref[0, 0], w1, preferred_element_type=F32)
    pe = pe_ref[0]
    pe_hi = pe.astype(BF16)
    pe_lo = (pe - pe_hi.astype(F32)).astype(BF16)
    pw = (jnp.dot(pe_hi, w1, preferred_element_type=F32) + jnp.dot(pe_lo, w1, preferred_element_type=F32))
    const = pw[0:1, :HEAD_DIM] + pw[1:2, HEAD_DIM:] + b1_ref[0]
    sh_ref[pl.ds(0, nb), :] = r[:, HEAD_DIM:]
    sh_ref[pl.ds(nb, SUBLANES), :] = jnp.zeros((SUBLANES, HEAD_DIM), F32)
    hid = r[:, :HEAD_DIM] + sh_ref[pl.ds(1, nb), :] + const
    out = jnp.dot(_gelu_tanh(hid).astype(BF16), w2_ref[0], preferred_element_type=F32) + b2_ref[0]
    row = lax.broadcasted_iota(jnp.int32, out.shape, 0)
    o_ref[0, 0, 0] = jnp.where(row < n_cmp, out, 0.0).astype(o_ref.dtype)


def _compress(kv16, w1cat, pe2, b1, w2, b2, batch, n_cmp):
    nb, wide = kv16.shape[2], kv16.shape[3]
    return pl.pallas_call(
        functools.partial(_compress_kernel, n_cmp=n_cmp),
        grid=(2, N_KV, batch),
        in_specs=[
            pl.BlockSpec((1, 1, nb, wide), lambda s, g, b: (s * N_KV + g, b, 0, 0)),
            pl.BlockSpec((1, wide, 2 * HEAD_DIM), lambda s, g, b: (s, 0, 0)),
            pl.BlockSpec((1, 2 * SUBLANES, wide), lambda s, g, b: (s, 0, 0)),
            pl.BlockSpec((1, 1, HEAD_DIM), lambda s, g, b: (s, 0, 0)),
            pl.BlockSpec((1, HEAD_DIM, HEAD_DIM), lambda s, g, b: (s, 0, 0)),
            pl.BlockSpec((1, 1, HEAD_DIM), lambda s, g, b: (s, 0, 0)),
        ],
        out_specs=pl.BlockSpec((1, 1, 1, nb, HEAD_DIM), lambda s, g, b: (s, b, g, 0, 0)),
        out_shape=jax.ShapeDtypeStruct((2, batch, N_KV, nb, HEAD_DIM), BF16),
        scratch_shapes=[pltpu.VMEM((nb + SUBLANES, HEAD_DIM), F32)],
        compiler_params=_cparams(("parallel", "parallel", "parallel")),
        name="nsa_compress",
    )(kv16, w1cat, pe2, b1, w2, b2)


FLAG_COL = HEAD_DIM + HEAD_DIM // 2
V_ROWS = HEAD_DIM + 16
KEY_TILE = 512
MASK_BLOCK = 128


def _blocked_probs(s, tq, kind_of, m_prev=None):
    nkb, nlt, nj = s.shape[0] // MASK_BLOCK, s.shape[1] // MASK_BLOCK, tq // MASK_BLOCK
    krow = lax.broadcasted_iota(jnp.int32, (MASK_BLOCK, MASK_BLOCK), 0)
    tcol = lax.broadcasted_iota(jnp.int32, (MASK_BLOCK, MASK_BLOCK), 1)
    visible = {"le": krow <= tcol, "gt": krow > tcol}
    zero = jnp.zeros((MASK_BLOCK, MASK_BLOCK), BF16)
    ms, cols = [], []
    for c in range(nlt):
        ls = slice(c * MASK_BLOCK, (c + 1) * MASK_BLOCK)
        blks = {}
        for kk in range(nkb):
            kind = kind_of(kk, c % nj)
            if kind is not None:
                b = s[kk * MASK_BLOCK:(kk + 1) * MASK_BLOCK, ls]
                blks[kk] = b if kind == "full" else jnp.where(visible[kind], b, NEG)
        m_c = jnp.max(functools.reduce(jnp.maximum, blks.values()), axis=0, keepdims=True)
        if m_prev is not None:
            m_c = jnp.maximum(m_c, m_prev[:, ls])
        ms.append(m_c)
        cols.append(jnp.concatenate(
            [jnp.exp((blks[kk] - m_c).astype(BF16)) if kk in blks else zero for kk in range(nkb)], axis=0))
    return jnp.concatenate(ms, axis=1), jnp.concatenate(cols, axis=1)


def _nsa_t_kernel(q_ref, z0_ref, z1_ref, z2_ref, gt_ref, kc_ref, vct_ref, ks_ref, vst_ref, kw_ref, vwt_ref,
                  ovt_ref, o_ref, ka_ref, kwa_ref, qa_ref, qw_ref, m_ref, acc_ref, ot_ref, s_ref, gz_ref,
                  *, tq, n_cmp, n_slc, topk):
    qi = pl.program_id(2)
    t0 = qi * tq
    rows = HPG * tq
    nsp = ovt_ref.shape[0]
    seq_len, aug = ka_ref.shape

    @pl.when(qi == 0)
    def _():
        key_blk = lax.broadcasted_iota(jnp.int32, (seq_len, aug - HEAD_DIM), 0) // SEL_LEN
        col = lax.broadcasted_iota(jnp.int32, (seq_len, aug - HEAD_DIM), 1)
        ka_ref[:, 0:HEAD_DIM] = ks_ref[0, 0]
        ka_ref[:, HEAD_DIM:] = jnp.where(key_blk == col, 1.0, 0.0).astype(BF16)
        pcol = lax.broadcasted_iota(jnp.int32, (WINDOW, aug), 1)
        kwa_ref[0:WINDOW, :] = jnp.where(pcol == FLAG_COL, 1.0, 0.0).astype(BF16)
        kwa_ref[WINDOW:, 0:HEAD_DIM] = kw_ref[0, 0]
        kwa_ref[WINDOW:, HEAD_DIM:] = jnp.zeros((seq_len, aug - HEAD_DIM), BF16)

    ones_rows = jnp.where(lax.broadcasted_iota(jnp.int32, (V_ROWS - HEAD_DIM, vst_ref.shape[4]), 0) == 0,
                          1.0, 0.0).astype(BF16)

    def v_tile(vt_ref, idx):
        return jnp.concatenate([vt_ref[0, 0, idx], ones_rows], axis=0)

    q = q_ref[...]
    qh = jnp.concatenate([q[:, h * HEAD_DIM:(h + 1) * HEAD_DIM] for h in range(HPG)], axis=0)
    flag = jnp.where(lax.broadcasted_iota(jnp.int32, (tq, aug - FLAG_COL), 1) == 0, NEG, 0.0).astype(BF16)
    qa_ref[:, 0:HEAD_DIM] = qh
    qw_ref[:, 0:HEAD_DIM] = qh
    qw_ref[:, HEAD_DIM:FLAG_COL] = jnp.zeros((rows, FLAG_COL - HEAD_DIM), BF16)
    for h in range(HPG):
        qa_ref[h * tq:(h + 1) * tq, FLAG_COL:] = flag
        qw_ref[h * tq:(h + 1) * tq, FLAG_COL:] = flag

    tk = vst_ref.shape[4]
    n_win = WINDOW // MASK_BLOCK

    def window_kind(kk, j):
        d = kk - j
        return "gt" if d == 0 else "le" if d == n_win else "full" if 0 < d < n_win else None

    sw = _dot_nt(kwa_ref[pl.ds(pl.multiple_of(t0, tq), WINDOW + tq), :], qw_ref[...])
    _, pw = _blocked_probs(sw, tq, window_kind)
    o_w = jnp.zeros((V_ROWS, rows), F32)
    for dd in range((WINDOW + tq) // tk):
        vidx = jnp.maximum(qi * (tq // tk) + dd - WINDOW // tk, 0)
        o_w = o_w + jnp.dot(v_tile(vwt_ref, vidx), pw[dd * tk:(dd + 1) * tk], preferred_element_type=F32)
    ot_ref[2] = o_w[:HEAD_DIM] * (1.0 / o_w[HEAD_DIM:HEAD_DIM + 1])

    gates = _sigmoid(gt_ref[...].astype(F32))
    for br, z_ref in enumerate((z0_ref, z1_ref, z2_ref)):
        for h in range(HPG):
            cs = slice(h * HEAD_DIM, (h + 1) * HEAD_DIM)
            z = z_ref[:, cs].astype(F32)
            c = br * HPG + h
            gz_ref[br, :, cs] = gates[:, c:c + 1] * (z * _sigmoid(z))

    ncp = kc_ref.shape[3]
    st = _dot_nt(kc_ref[0, 0, 0], qh)
    n_idx = lax.broadcasted_iota(jnp.int32, (ncp, rows), 0)
    tpos = t0 + lax.broadcasted_iota(jnp.int32, (ncp, rows), 1) % tq
    cmask = (n_idx * CMP_STRIDE + (CMP_LEN - 1) <= tpos) & (n_idx < n_cmp)
    st = jnp.where(cmask, st, NEG)
    e = jnp.where(cmask, jnp.exp(st - jnp.max(st, axis=0, keepdims=True)), 0.0)
    l = jnp.sum(e, axis=0, keepdims=True)
    p = e * (1.0 / jnp.where(l > 0.0, l, 1.0))
    ot_ref[0] = jnp.dot(vct_ref[0, 0], p.astype(BF16), preferred_element_type=F32)
    p_cmp = p[:, 0:tq]
    for h in range(1, HPG):
        p_cmp = p_cmp + p[:, h * tq:(h + 1) * tq]

    p_hi = p_cmp.astype(BF16)
    r1 = p_cmp - p_hi.astype(F32)
    p_mid = r1.astype(BF16)
    p_lo = (r1 - p_mid.astype(F32)).astype(BF16)
    ovt = ovt_ref[...]
    p_slc = (jnp.dot(ovt, p_hi, preferred_element_type=F32) + jnp.dot(ovt, p_mid, preferred_element_type=F32)
             + jnp.dot(ovt, p_lo, preferred_element_type=F32))
    blk = lax.broadcasted_iota(jnp.int32, (nsp, tq), 0)
    cur = (t0 + lax.broadcasted_iota(jnp.int32, (nsp, tq), 1)) // SEL_LEN
    valid = (blk <= cur) & (blk < n_slc)
    forced = (blk == 0) | (blk == cur) | (blk == cur - 1)
    score = jnp.where(valid, p_slc + jnp.where(forced, SEL_BONUS, 0.0), -SEL_BONUS)
    nv = nsp // SUBLANES
    sc = [score[k * SUBLANES:(k + 1) * SUBLANES] for k in range(nv)]
    sub = lax.broadcasted_iota(jnp.int32, (SUBLANES, tq), 0)
    rank = [jnp.zeros((SUBLANES, tq), jnp.int32) for _ in range(nv)]
    for i in range(n_slc):
        si = jnp.broadcast_to(score[i:i + 1, :], (SUBLANES, tq))
        for k in range(nv):
            if k * SUBLANES > i:
                beats = si >= sc[k]
            elif k * SUBLANES + SUBLANES - 1 <= i:
                beats = si > sc[k]
            else:
                beats = (si > sc[k]) | ((si == sc[k]) & (sub > i - k * SUBLANES))
            rank[k] = rank[k] + jnp.where(beats, 1, 0)
    rank = jnp.concatenate(rank, axis=0)
    sel = (rank < topk) & (score > -0.5 * SEL_BONUS)
    selb = jnp.where(sel, 0.0, NEG).T.astype(BF16)
    for h in range(HPG):
        qa_ref[h * tq:(h + 1) * tq, HEAD_DIM:FLAG_COL] = selb


    def scores(k_tile):
        return _dot_nt(k_tile, qa_ref[...])

    def online_update(s, vt_tile):
        m_old = m_ref[...]
        m_new = jnp.maximum(m_old, jnp.max(s, axis=0, keepdims=True))
        alpha = jnp.exp(m_old - m_new)
        pt = jnp.exp((s - m_new).astype(BF16))
        acc_ref[...] = alpha * acc_ref[...] + jnp.dot(vt_tile, pt, preferred_element_type=F32)
        m_ref[...] = m_new

    m_ref[...] = jnp.full(m_ref.shape, NEG, F32)
    acc_ref[...] = jnp.zeros(acc_ref.shape, F32)
    n_full = t0 // tk
    s_ref[...] = scores(ka_ref[pl.ds(0, tk), :])

    def sel_body(kt, carry):
        k1 = pl.multiple_of((kt + 1) * tk, tk)
        s_next = scores(ka_ref[pl.ds(k1, tk), :])
        online_update(s_ref[...], v_tile(vst_ref, kt))
        s_ref[...] = s_next
        return carry

    lax.fori_loop(0, n_full, sel_body, 0)

    def diagonal_kind(kk, j):
        return "full" if kk < j else "le" if kk == j else None

    m_old = m_ref[...]
    m_new, pt = _blocked_probs(s_ref[...], tq, diagonal_kind, m_old)
    acc = jnp.exp(m_old - m_new) * acc_ref[...] + jnp.dot(v_tile(vst_ref, n_full), pt, preferred_element_type=F32)
    ot_ref[1] = acc[:HEAD_DIM] * (1.0 / acc[HEAD_DIM:HEAD_DIM + 1])

    for h in range(HPG):
        cs = slice(h * HEAD_DIM, (h + 1) * HEAD_DIM)
        tot = None
        for br in range(N_BRANCH):
            term = gz_ref[br, :, cs] * ot_ref[br, :, h * tq:(h + 1) * tq].T
            tot = term if tot is None else tot + term
        o_ref[:, cs] = tot.astype(o_ref.dtype)


def _nsa_attention_t(proj, kcv, vct, kslab, vt, ovt, batch, seq_len, *, tq):
    m = proj.shape[0]
    gw = HPG * HEAD_DIM
    qt = seq_len // tq
    n_cmp = (seq_len - CMP_LEN) // CMP_STRIDE + 1
    n_slc = seq_len // SEL_LEN
    z_base = N_KV
    gate_base = (N_KV + N_BRANCH * N_KV) * (gw // LANES)
    ncp = kcv.shape[3]
    aug = 2 * HEAD_DIM
    rows = HPG * tq
    tk = vt.shape[-1]

    def zspec(br):
        return pl.BlockSpec((tq, gw), lambda b, g, i: (b * qt + i, z_base + br * N_KV + g))

    def per_group(arr, kind):
        return pl.BlockSpec((1, 1) + arr.shape[2:], lambda b, g, i: (kind * N_KV + g, b) + (0,) * (arr.ndim - 2))

    return pl.pallas_call(
        functools.partial(_nsa_t_kernel, tq=tq, n_cmp=n_cmp, n_slc=n_slc, topk=min(SEL_TOPK, n_slc)),
        grid=(batch, N_KV, qt),
        in_specs=[
            pl.BlockSpec((tq, gw), lambda b, g, i: (b * qt + i, g)),
            zspec(0), zspec(1), zspec(2),
            pl.BlockSpec((tq, LANES), lambda b, g, i: (b * qt + i, gate_base + g)),
            pl.BlockSpec((1, 1, 1, ncp, HEAD_DIM), lambda b, g, i: (0, b, g, 0, 0)),
            pl.BlockSpec((1, 1, HEAD_DIM, ncp), lambda b, g, i: (b, g, 0, 0)),
            per_group(kslab, KV_KSEL), per_group(vt, VT_SEL), per_group(kslab, KV_KWIN), per_group(vt, VT_WIN),
            pl.BlockSpec(ovt.shape, lambda b, g, i: (0, 0)),
        ],
        out_specs=pl.BlockSpec((tq, gw), lambda b, g, i: (b * qt + i, g)),
        out_shape=jax.ShapeDtypeStruct((m, N_KV * gw), BF16),
        scratch_shapes=[
            pltpu.VMEM((seq_len, aug), BF16),
            pltpu.VMEM((seq_len + WINDOW, aug), BF16),
            pltpu.VMEM((rows, aug), BF16),
            pltpu.VMEM((rows, aug), BF16),
            pltpu.VMEM((1, rows), F32),
            pltpu.VMEM((V_ROWS, rows), F32),
            pltpu.VMEM((N_BRANCH, HEAD_DIM, rows), F32),
            pltpu.VMEM((tk, rows), F32),
            pltpu.VMEM((N_BRANCH, tq, gw), F32),
        ],
        compiler_params=_cparams(("parallel", "parallel", "arbitrary")),
        name="nsa_attention",
    )(proj, proj, proj, proj, proj, kcv, vct, kslab, vt, kslab, vt, ovt)


def _selection_overlap_t(n_cmp, ncp, n_slc, nsp):
    c0 = np.arange(ncp)[None, :] * CMP_STRIDE
    s0 = np.arange(nsp)[:, None] * SEL_LEN
    ov = np.clip(np.minimum(c0 + CMP_LEN, s0 + SEL_LEN) - np.maximum(c0, s0), 0, None) / CMP_STRIDE
    ov = ov * (np.arange(ncp)[None, :] < n_cmp) * (np.arange(nsp)[:, None] < n_slc)
    return jnp.asarray(ov, dtype=BF16)


def _s5_layer(x2d, mods, norm_g, w_in, s5p, d_skip, w_glu, b_glu, w_out, batch, seq_len, final_g, final_norm):
    d = x2d.shape[1]
    shift, scale, gate = (mods[:, None, i * d:(i + 1) * d] for i in range(3))
    e = w_glu.shape[0]
    uz = _normmod_matmul(x2d, norm_g[None], shift, scale, w_in.astype(BF16), jnp.ones((1, 2 * e), F32),
                         seq_len, tm=PROJ_ROWS, tn=1024)
    wb, wc, a_re, a_im = s5p
    y = _s5_scan(uz, wb, wc, a_re, a_im, d_skip[None].astype(F32), batch, seq_len, steps=256)
    return _glu_out_residual(y, uz, w_glu.astype(BF16), b_glu[None].astype(F32), w_out.astype(BF16), x2d, gate,
                             final_g[None], seq_len, tm=512, final_norm=final_norm)


def _qg_weight(w_qg):
    d = w_qg.shape[0]
    att = N_HEADS * HEAD_DIM
    g_end = att + N_BRANCH * N_HEADS
    wg = w_qg[:, att:g_end].reshape(d, N_BRANCH, N_KV, HPG)
    wg = jnp.transpose(wg, (0, 2, 1, 3)).reshape(d, N_KV, N_BRANCH * HPG)
    wg = jnp.pad(wg, ((0, 0), (0, 0), (0, LANES - N_BRANCH * HPG))).reshape(d, N_KV * LANES)
    w = jnp.concatenate([w_qg[:, :att], w_qg[:, g_end:], wg], axis=1).astype(BF16)
    cs = jnp.concatenate([jnp.full((att,), HEAD_DIM ** -0.5, F32), jnp.ones((w.shape[1] - att,), F32)])
    return w, cs[None]


def kernel(x, c, norm_g, mod_w, mod_b, ssm_w_in, ssm_lam_re, ssm_lam_im, ssm_log_step, ssm_b_re, ssm_b_im, ssm_c_re, ssm_c_im, ssm_d, ssm_w_glu, ssm_b_glu, ssm_w_out, kv_norm_g, kv_mod_w, kv_mod_b, w_kv, cmp_pe, cmp_w1, cmp_b1, cmp_w2, cmp_b2, nsa_w_qg, nsa_w_o, final_norm_g):
    batch, seq_len, d = x.shape
    depth = mod_w.shape[0]
    n_a = ssm_w_in.shape[0]
    m = batch * seq_len
    x2d = x.reshape(m, d)

    c_pad = jnp.pad(c, ((0, 2 * SUBLANES - batch), (0, 0)))
    mods = _cond_matmul(c_pad, mod_w, mod_b[:, None])[:, :batch]
    kv_mods = _cond_matmul(c_pad, kv_mod_w[None], kv_mod_b[None, None])[0, :batch]

    for layer in range(n_a):
        s5p = _s5_params(ssm_lam_re[layer], ssm_lam_im[layer], ssm_log_step[layer], ssm_b_re[layer],
                         ssm_b_im[layer], ssm_c_re[layer], ssm_c_im[layer])
        x2d = _s5_layer(x2d, mods[layer], norm_g[layer], ssm_w_in[layer], s5p, ssm_d[layer], ssm_w_glu[layer],
                        ssm_b_glu[layer], ssm_w_out[layer], batch, seq_len, final_norm_g,
                        final_norm=(layer == depth - 1))

    tq = KEY_TILE
    assert seq_len % KEY_TILE == 0 and WINDOW % KEY_TILE == 0 and WINDOW % MASK_BLOCK == 0
    kv_shift, kv_scale = kv_mods[:, None, :d], kv_mods[:, None, d:]
    kslab, vt = _kv_projection(x2d, kv_norm_g[None], kv_shift, kv_scale, w_kv.astype(BF16), batch, seq_len,
                               tm=PROJ_ROWS, tk=KEY_TILE)
    kslab = kslab.reshape(4 * N_KV, batch, seq_len, HEAD_DIM)

    n_cmp = (seq_len - CMP_LEN) // CMP_STRIDE + 1
    n_slc = seq_len // SEL_LEN
    nb16 = seq_len // CMP_STRIDE
    half = CMP_STRIDE * HEAD_DIM
    kv16 = kslab[:2 * N_KV].reshape(2 * N_KV, batch, nb16, half)
    w1cat = jnp.concatenate([cmp_w1[:, :half], cmp_w1[:, half:]], axis=2).astype(BF16)
    pe2 = jnp.pad(cmp_pe.reshape(2, 2, half), ((0, 0), (0, 2 * SUBLANES - 2), (0, 0)))
    kcv = _compress(kv16, w1cat, pe2, cmp_b1[:, None], cmp_w2.astype(BF16), cmp_b2[:, None], batch, n_cmp)

    nsp = FLAG_COL - HEAD_DIM
    assert n_slc <= nsp
    vct = jnp.swapaxes(kcv[1], -1, -2)
    ovt = _selection_overlap_t(n_cmp, nb16, n_slc, nsp)

    for layer in range(n_a, depth):
        j = layer - n_a
        shift, scale, gate = (mods[layer][:, None, i * d:(i + 1) * d] for i in range(3))
        wq, cs = _qg_weight(nsa_w_qg[j])
        proj = _normmod_matmul(x2d, norm_g[layer][None], shift, scale, wq, cs, seq_len, tm=PROJ_ROWS, tn=512)
        o = _nsa_attention_t(proj, kcv, vct, kslab, vt, ovt, batch, seq_len, tq=tq)
        x2d = _mm_residual(o, nsa_w_o[j].astype(BF16), x2d, gate, final_norm_g[None], seq_len, tm=512,
                           final_norm=(layer == depth - 1))

    return x2d.reshape(batch, seq_len, d)
```

```python
import functools
import math

import jax
import jax.numpy as jnp
import numpy as np
from jax import lax
from jax.experimental import pallas as pl
from jax.experimental.pallas import tpu as pltpu

F32 = jnp.float32
BF16 = jnp.bfloat16

SSM_GROUP = 16
SSM_STATE = 64
N_HEADS = 16
N_KV = 4
HPG = N_HEADS // N_KV
HEAD_DIM = 128
N_BRANCH = 3
CMP_LEN = 32
CMP_STRIDE = 16
SEL_LEN = 64
SEL_TOPK = 16
WINDOW = 512
SEL_BONUS = 1e3
NEG = -1e30
EPS = 1e-6

LANES = 128
SUBLANES = 8
MXU_DIM = 256
VMEM_LIMIT = 56 * 1024 * 1024

PROJ_ROWS = 1024

GROUPS_PER_BLOCK = MXU_DIM // SSM_GROUP
STATES_PER_BLOCK = GROUPS_PER_BLOCK * SSM_STATE


def _cparams(sem):
    return pltpu.CompilerParams(dimension_semantics=sem, vmem_limit_bytes=VMEM_LIMIT)


def _gelu_tanh(x):
    return x * (0.5 * (1.0 + jnp.tanh(math.sqrt(2.0 / math.pi) * (x + 0.044715 * (x * x * x)))))


def _sigmoid(x):
    return 0.5 * jnp.tanh(0.5 * x) + 0.5


def _dot_nt(a, b):
    return lax.dot_general(a, b, (((1,), (1,)), ((), ())), preferred_element_type=F32)


def _cond_kernel(c_ref, w_ref, b_ref, o_ref):
    c = c_ref[...]
    ca = (c * _sigmoid(c)).astype(BF16)
    acc = jnp.dot(ca, w_ref[0].astype(BF16), preferred_element_type=F32)
    o_ref[0] = acc + b_ref[0]


def _cond_matmul(c_pad, w, b, tn=512):
    nl, d, n = w.shape
    r = c_pad.shape[0]
    return pl.pallas_call(
        _cond_kernel,
        grid=(nl, n // tn),
        in_specs=[
            pl.BlockSpec((r, d), lambda l, j: (0, 0)),
            pl.BlockSpec((1, d, tn), lambda l, j: (l, 0, j)),
            pl.BlockSpec((1, 1, tn), lambda l, j: (l, 0, j)),
        ],
        out_specs=pl.BlockSpec((1, r, tn), lambda l, j: (l, 0, j)),
        out_shape=jax.ShapeDtypeStruct((nl, r, n), F32),
        compiler_params=_cparams(("parallel", "parallel")),
        name="cond_matmul",
    )(c_pad, w, b)


NORM_SLAB = 16


def _norm_modulate(x_ref, g_ref, sh_ref, sc_ref, h_ref):
    gain = g_ref[...] * (1.0 + sc_ref[0])
    shift = sh_ref[0]

    def slab(r, carry):
        rs = pl.ds(pl.multiple_of(r * NORM_SLAB, NORM_SLAB), NORM_SLAB)
        x = x_ref[rs, :]
        y = x * lax.rsqrt(jnp.mean(x * x, axis=-1, keepdims=True) + EPS)
        h_ref[rs, :] = (y * gain + shift).astype(BF16)
        return carry

    lax.fori_loop(0, x_ref.shape[0] // NORM_SLAB, slab, 0, unroll=8)


def _normmod_mm_kernel(x_ref, g_ref, sh_ref, sc_ref, w_ref, cs_ref, o_ref, h_ref):
    @pl.when(pl.program_id(1) == 0)
    def _():
        _norm_modulate(x_ref, g_ref, sh_ref, sc_ref, h_ref)

    acc = jnp.dot(h_ref[...], w_ref[...], preferred_element_type=F32) * cs_ref[...]
    o_ref[...] = acc.astype(o_ref.dtype)


def _normmod_matmul(x2d, g, shift, scale, w, layer, colscale, seq_len, *, tm, tn):
    m, d = x2d.shape
    n = w.shape[2]
    rows_per_batch = seq_len // tm
    return pl.pallas_call(
        _normmod_mm_kernel,
        grid=(m // tm, n // tn),
        in_specs=[
            pl.BlockSpec((tm, d), lambda i, j: (i, 0)),
            pl.BlockSpec((1, d), lambda i, j: (0, 0)),
            pl.BlockSpec((1, 1, d), lambda i, j: (i // rows_per_batch, 0, 0)),
            pl.BlockSpec((1, 1, d), lambda i, j: (i // rows_per_batch, 0, 0)),
            pl.BlockSpec((None, d, tn), lambda i, j: (layer, 0, j)),
            pl.BlockSpec((1, tn), lambda i, j: (0, j)),
        ],
        out_specs=pl.BlockSpec((tm, tn), lambda i, j: (i, j)),
        out_shape=jax.ShapeDtypeStruct((m, n), BF16),
        scratch_shapes=[pltpu.VMEM((tm, d), BF16)],
        compiler_params=_cparams(("parallel", "arbitrary")),
        name="normmod_matmul",
    )(x2d, g, shift, scale, w, colscale)


KV_KCMP, KV_VCMP, KV_KSEL, KV_KWIN = range(4)
VT_SEL, VT_WIN = range(2)
_SEL_V_COL, _WIN_V_COL = 3, 5


def _kv_proj_kernel(x_ref, g_ref, sh_ref, sc_ref, w_ref, ks_ref, vt_ref, h_ref):
    j = pl.program_id(1)

    @pl.when(j == 0)
    def _():
        _norm_modulate(x_ref, g_ref, sh_ref, sc_ref, h_ref)

    acc = jnp.dot(h_ref[...], w_ref[...], preferred_element_type=F32)
    tk = vt_ref.shape[-1]
    is_value = (j == _SEL_V_COL) | (j == _WIN_V_COL)

    @pl.when(jnp.logical_not(is_value))
    def _():
        for s in range(N_KV):
            ks_ref[s] = acc[:, s * HEAD_DIM:(s + 1) * HEAD_DIM].astype(ks_ref.dtype)

    @pl.when(is_value)
    def _():
        for s in range(N_KV):
            for r in range(acc.shape[0] // tk):
                blk = acc[r * tk:(r + 1) * tk, s * HEAD_DIM:(s + 1) * HEAD_DIM]
                vt_ref[s, 0, r] = blk.T.astype(vt_ref.dtype)


def _kv_projection(x2d, g, shift, scale, w, batch, seq_len, *, tm, tk):
    m, d = x2d.shape
    tn = N_KV * HEAD_DIM
    assert w.shape[1] == 6 * tn
    rpb = seq_len // tm

    def slab_kind(j):
        return j - (j >= _SEL_V_COL).astype(jnp.int32) - (j >= _WIN_V_COL).astype(jnp.int32)

    return pl.pallas_call(
        _kv_proj_kernel,
        grid=(m // tm, 6),
        in_specs=[
            pl.BlockSpec((tm, d), lambda i, j: (i, 0)),
            pl.BlockSpec((1, d), lambda i, j: (0, 0)),
            pl.BlockSpec((1, 1, d), lambda i, j: (i // rpb, 0, 0)),
            pl.BlockSpec((1, 1, d), lambda i, j: (i // rpb, 0, 0)),
            pl.BlockSpec((d, tn), lambda i, j: (0, j)),
        ],
        out_specs=[
            pl.BlockSpec((N_KV, tm, HEAD_DIM), lambda i, j: (slab_kind(j), i, 0)),
            pl.BlockSpec((N_KV, 1, tm // tk, HEAD_DIM, tk),
                         lambda i, j: ((j > _SEL_V_COL).astype(jnp.int32), i // rpb, i % rpb, 0, 0)),
        ],
        out_shape=[
            jax.ShapeDtypeStruct((4 * N_KV, m, HEAD_DIM), BF16),
            jax.ShapeDtypeStruct((2 * N_KV, batch, seq_len // tk, HEAD_DIM, tk), BF16),
        ],
        scratch_shapes=[pltpu.VMEM((tm, d), BF16)],
        compiler_params=_cparams(("parallel", "arbitrary")),
        name="kv_projection",
    )(x2d, g, shift, scale, w)


def _mm_res_kernel(a_ref, w_ref, x_ref, gate_ref, fg_ref, o_ref, *, final_norm):
    acc = jnp.dot(a_ref[...], w_ref[...], preferred_element_type=F32)
    xn = x_ref[...] + gate_ref[0] * acc
    if final_norm:
        xn = xn * lax.rsqrt(jnp.mean(xn * xn, axis=-1, keepdims=True) + EPS) * fg_ref[...]
    o_ref[...] = xn


def _mm_residual(a, w, layer, x2d, gate, final_g, seq_len, *, tm, final_norm):
    m, k = a.shape
    n = w.shape[2]
    rows_per_batch = seq_len // tm
    return pl.pallas_call(
        functools.partial(_mm_res_kernel, final_norm=final_norm),
        grid=(m // tm,),
        in_specs=[
            pl.BlockSpec((tm, k), lambda i: (i, 0)),
            pl.BlockSpec((None, k, n), lambda i: (layer, 0, 0), pipeline_mode=pl.Buffered(1)),
            pl.BlockSpec((tm, n), lambda i: (i, 0)),
            pl.BlockSpec((1, 1, n), lambda i: (i // rows_per_batch, 0, 0)),
            pl.BlockSpec((1, n), lambda i: (0, 0)),
        ],
        out_specs=pl.BlockSpec((tm, n), lambda i: (i, 0)),
        out_shape=jax.ShapeDtypeStruct((m, n), F32),
        compiler_params=_cparams(("parallel",)),
        name="matmul_residual_final" if final_norm else "matmul_residual",
    )(a, w, x2d, gate, final_g)


def _glu_out_kernel(y_ref, z_ref, wg_ref, bg_ref, wo_ref, x_ref, gate_ref, fg_ref, o_ref, *, final_norm):
    y = y_ref[...]
    lin = jnp.dot(y, wg_ref[...], preferred_element_type=F32) + bg_ref[...]
    z = z_ref[...].astype(F32)
    t = ((y.astype(F32) * _sigmoid(lin)) * (z * _sigmoid(z))).astype(BF16)
    xn = x_ref[...] + gate_ref[0] * jnp.dot(t, wo_ref[...], preferred_element_type=F32)
    if final_norm:
        xn = xn * lax.rsqrt(jnp.mean(xn * xn, axis=-1, keepdims=True) + EPS) * fg_ref[...]
    o_ref[...] = xn


def _glu_out_residual(y, uz, w_glu, b_glu, w_out, layer, x2d, gate, final_g, seq_len, *, tm, final_norm):
    m, e = y.shape
    n = w_out.shape[2]
    rows_per_batch = seq_len // tm
    resident = pl.Buffered(1)
    return pl.pallas_call(
        functools.partial(_glu_out_kernel, final_norm=final_norm),
        grid=(m // tm,),
        in_specs=[
            pl.BlockSpec((tm, e), lambda i: (i, 0)),
            pl.BlockSpec((tm, e), lambda i: (i, 1)),
            pl.BlockSpec((None, e, e), lambda i: (layer, 0, 0), pipeline_mode=resident),
            pl.BlockSpec((1, e), lambda i: (0, 0)),
            pl.BlockSpec((None, e, n), lambda i: (layer, 0, 0), pipeline_mode=resident),
            pl.BlockSpec((tm, n), lambda i: (i, 0)),
            pl.BlockSpec((1, 1, n), lambda i: (i // rows_per_batch, 0, 0)),
            pl.BlockSpec((1, n), lambda i: (0, 0)),
        ],
        out_specs=pl.BlockSpec((tm, n), lambda i: (i, 0)),
        out_shape=jax.ShapeDtypeStruct((m, n), F32),
        compiler_params=_cparams(("parallel",)),
        name="s5_glu_out",
    )(y, uz, w_glu, b_glu, w_out, x2d, gate, final_g)


def _s5_scan_kernel(u_ref, wb_ref, wc_ref, are_ref, aim_ref, d_ref, y_ref, s_ref, st_ref, *, steps, nblk):
    nslab = STATES_PER_BLOCK // LANES

    @pl.when(pl.program_id(1) == 0)
    def _():
        st_ref[...] = jnp.zeros_like(st_ref)

    for blk in range(nblk):
        bu = jnp.dot(u_ref[:, blk * MXU_DIM:(blk + 1) * MXU_DIM], wb_ref[blk], preferred_element_type=F32)
        for k in range(2 * nslab):
            s_ref[k, pl.ds(blk, steps, stride=nblk), :] = bu[:, k * LANES:(k + 1) * LANES]

    a_re = [are_ref[:, k * LANES:(k + 1) * LANES] for k in range(nslab)]
    a_im = [aim_ref[:, k * LANES:(k + 1) * LANES] for k in range(nslab)]

    def step(t, carry):
        xr, xi = carry
        r0 = pl.multiple_of(t * nblk, nblk)
        nr, ni = [], []
        for k in range(nslab):
            b_re = s_ref[k, pl.ds(r0, nblk), :]
            b_im = s_ref[nslab + k, pl.ds(r0, nblk), :]
            v_re = a_re[k] * xr[k] - a_im[k] * xi[k] + b_re
            v_im = a_re[k] * xi[k] + a_im[k] * xr[k] + b_im
            s_ref[k, pl.ds(r0, nblk), :] = v_re
            s_ref[nslab + k, pl.ds(r0, nblk), :] = v_im
            nr.append(v_re)
            ni.append(v_im)
        return tuple(nr), tuple(ni)

    init = (tuple(st_ref[k] for k in range(nslab)), tuple(st_ref[nslab + k] for k in range(nslab)))
    xr, xi = lax.fori_loop(0, steps, step, init, unroll=2)
    for k in range(nslab):
        st_ref[k] = xr[k]
        st_ref[nslab + k] = xi[k]

    for blk in range(nblk):
        xs = jnp.concatenate(
            [s_ref[k, pl.ds(blk, steps, stride=nblk), :].astype(BF16) for k in range(2 * nslab)], axis=1)
        cols = slice(blk * MXU_DIM, (blk + 1) * MXU_DIM)
        yb = jnp.dot(xs, wc_ref[blk], preferred_element_type=F32)
        yb = yb + d_ref[:, cols] * u_ref[:, cols].astype(F32)
        y_ref[:, cols] = _gelu_tanh(yb).astype(y_ref.dtype)


def _s5_scan(uz, wb, wc, a_re, a_im, d_skip, batch, seq_len, *, steps):
    m = uz.shape[0]
    nblk, _, two_states = wb.shape
    e = nblk * MXU_DIM
    chunks = seq_len // steps
    nslab2 = two_states // LANES
    return pl.pallas_call(
        functools.partial(_s5_scan_kernel, steps=steps, nblk=nblk),
        grid=(batch, chunks),
        in_specs=[
            pl.BlockSpec((steps, e), lambda b, c: (b * chunks + c, 0)),
            pl.BlockSpec(wb.shape, lambda b, c: (0, 0, 0), pipeline_mode=pl.Buffered(1)),
            pl.BlockSpec(wc.shape, lambda b, c: (0, 0, 0), pipeline_mode=pl.Buffered(1)),
            pl.BlockSpec(a_re.shape, lambda b, c: (0, 0)),
            pl.BlockSpec(a_im.shape, lambda b, c: (0, 0)),
            pl.BlockSpec((1, e), lambda b, c: (0, 0)),
        ],
        out_specs=pl.BlockSpec((steps, e), lambda b, c: (b * chunks + c, 0)),
        out_shape=jax.ShapeDtypeStruct((m, e), BF16),
        scratch_shapes=[
            pltpu.VMEM((nslab2, steps * nblk, LANES), F32),
            pltpu.VMEM((nslab2, nblk, LANES), F32),
        ],
        compiler_params=_cparams(("parallel", "arbitrary")),
        name="s5_scan",
    )(uz, wb, wc, a_re, a_im, d_skip)


def _s5_params(lam_re, lam_im, log_step, b_re, b_im, c_re, c_im):
    g, n = lam_re.shape
    dt = jnp.exp(log_step.astype(F32))[:, None]
    lr, li = lam_re.astype(F32), lam_im.astype(F32)
    mag = jnp.exp(lr * dt)
    a_re, a_im = mag * jnp.cos(li * dt), mag * jnp.sin(li * dt)
    den = lr * lr + li * li
    coef_re = ((a_re - 1.0) * lr + a_im * li) / den
    coef_im = (a_im * lr - (a_re - 1.0) * li) / den
    br, bi = b_re.astype(F32), b_im.astype(F32)
    bb_re = coef_re[..., None] * br - coef_im[..., None] * bi
    bb_im = coef_re[..., None] * bi + coef_im[..., None] * br
    nblk = g // GROUPS_PER_BLOCK

    def blockdiag(t, inner):
        rows = t.shape[1]
        cols = GROUPS_PER_BLOCK * inner
        expand = (jnp.arange(cols)[None, :] % inner == jnp.arange(inner)[:, None]).astype(F32)
        tiled = jnp.einsum('brn,nk->brk', t, expand, precision=lax.Precision.HIGHEST)
        same_group = (jnp.arange(rows)[:, None] // (rows // GROUPS_PER_BLOCK)
                      == jnp.arange(cols)[None, :] // inner)
        return jnp.where(same_group, tiled, 0.0)

    def blockdiag_in(bb):
        t = jnp.swapaxes(bb.reshape(nblk, GROUPS_PER_BLOCK, n, SSM_GROUP), 2, 3)
        return blockdiag(t.reshape(nblk, GROUPS_PER_BLOCK * SSM_GROUP, n), n)

    def blockdiag_out(cc):
        t = jnp.swapaxes(cc.astype(F32).reshape(nblk, GROUPS_PER_BLOCK, SSM_GROUP, n), 2, 3)
        return blockdiag(t.reshape(nblk, GROUPS_PER_BLOCK * n, SSM_GROUP), SSM_GROUP)

    wb = jnp.concatenate([blockdiag_in(bb_re), blockdiag_in(bb_im)], axis=2).astype(BF16)
    wc = jnp.concatenate([blockdiag_out(c_re), -blockdiag_out(c_im)], axis=1).astype(BF16)
    return wb, wc, a_re.reshape(nblk, GROUPS_PER_BLOCK * n), a_im.reshape(nblk, GROUPS_PER_BLOCK * n)


def _compress_kernel(x_ref, w1_ref, pe_ref, b1_ref, w2_ref, b2_ref, o_ref, sh_ref, *, n_cmp):
    nb = x_ref.shape[2]
    w1 = w1_ref[0]
    r = jnp.dot(x_ref[0, 0], w1, preferred_element_type=F32)
    pe = pe_ref[0]
    pe_hi = pe.astype(BF16)
    pe_lo = (pe - pe_hi.astype(F32)).astype(BF16)
    pw = (jnp.dot(pe_hi, w1, preferred_element_type=F32) + jnp.dot(pe_lo, w1, preferred_element_type=F32))
    const = pw[0:1, :HEAD_DIM] + pw[1:2, HEAD_DIM:] + b1_ref[0]
    sh_ref[pl.ds(0, nb), :] = r[:, HEAD_DIM:]
    sh_ref[pl.ds(nb, SUBLANES), :] = jnp.zeros((SUBLANES, HEAD_DIM), F32)
    hid = r[:, :HEAD_DIM] + sh_ref[pl.ds(1, nb), :] + const
    out = jnp.dot(_gelu_tanh(hid).astype(BF16), w2_ref[0], preferred_element_type=F32) + b2_ref[0]
    row = lax.broadcasted_iota(jnp.int32, out.shape, 0)
    o_ref[0, 0, 0] = jnp.where(row < n_cmp, out, 0.0).astype(o_ref.dtype)


def _compress(kv16, w1cat, pe2, b1, w2, b2, batch, n_cmp):
    nb, wide = kv16.shape[2], kv16.shape[3]
    return pl.pallas_call(
        functools.partial(_compress_kernel, n_cmp=n_cmp),
        grid=(2, N_KV, batch),
        in_specs=[
            pl.BlockSpec((1, 1, nb, wide), lambda s, g, b: (s * N_KV + g, b, 0, 0)),
            pl.BlockSpec((1, wide, 2 * HEAD_DIM), lambda s, g, b: (s, 0, 0)),
            pl.BlockSpec((1, 2 * SUBLANES, wide), lambda s, g, b: (s, 0, 0)),
            pl.BlockSpec((1, 1, HEAD_DIM), lambda s, g, b: (s, 0, 0)),
            pl.BlockSpec((1, HEAD_DIM, HEAD_DIM), lambda s, g, b: (s, 0, 0)),
            pl.BlockSpec((1, 1, HEAD_DIM), lambda s, g, b: (s, 0, 0)),
        ],
        out_specs=pl.BlockSpec((1, 1, 1, nb, HEAD_DIM), lambda s, g, b: (s, b, g, 0, 0)),
        out_shape=jax.ShapeDtypeStruct((2, batch, N_KV, nb, HEAD_DIM), BF16),
        scratch_shapes=[pltpu.VMEM((nb + SUBLANES, HEAD_DIM), F32)],
        compiler_params=_cparams(("parallel", "parallel", "parallel")),
        name="nsa_compress",
    )(kv16, w1cat, pe2, b1, w2, b2)


FLAG_COL = HEAD_DIM + HEAD_DIM // 2
V_ROWS = HEAD_DIM + 16
KEY_TILE = 512
MASK_BLOCK = 128


def _blocked_probs(s, tq, kind_of, m_prev=None):
    nkb, nlt, nj = s.shape[0] // MASK_BLOCK, s.shape[1] // MASK_BLOCK, tq // MASK_BLOCK
    krow = lax.broadcasted_iota(jnp.int32, (MASK_BLOCK, MASK_BLOCK), 0)
    tcol = lax.broadcasted_iota(jnp.int32, (MASK_BLOCK, MASK_BLOCK), 1)
    visible = {"le": krow <= tcol, "gt": krow > tcol}
    zero = jnp.zeros((MASK_BLOCK, MASK_BLOCK), BF16)
    ms, cols = [], []
    for c in range(nlt):
        ls = slice(c * MASK_BLOCK, (c + 1) * MASK_BLOCK)
        blks = {}
        for kk in range(nkb):
            kind = kind_of(kk, c % nj)
            if kind is not None:
                b = s[kk * MASK_BLOCK:(kk + 1) * MASK_BLOCK, ls]
                blks[kk] = b if kind == "full" else jnp.where(visible[kind], b, NEG)
        m_c = jnp.max(functools.reduce(jnp.maximum, blks.values()), axis=0, keepdims=True)
        if m_prev is not None:
            m_c = jnp.maximum(m_c, m_prev[:, ls])
        ms.append(m_c)
        cols.append(jnp.concatenate(
            [jnp.exp((blks[kk] - m_c).astype(BF16)) if kk in blks else zero for kk in range(nkb)], axis=0))
    return jnp.concatenate(ms, axis=1), jnp.concatenate(cols, axis=1)


def _nsa_t_kernel(q_ref, z0_ref, z1_ref, z2_ref, gt_ref, kc_ref, vct_ref, ks_ref, vst_ref, kw_ref, vwt_ref,
                  ovt_ref, o_ref, ka_ref, kwa_ref, qa_ref, qw_ref, m_ref, acc_ref, ot_ref, s_ref, gz_ref,
                  *, tq, n_cmp, n_slc, topk):
    qi = pl.program_id(2)
    t0 = qi * tq
    rows = HPG * tq
    nsp = ovt_ref.shape[0]
    seq_len, aug = ka_ref.shape

    @pl.when(qi == 0)
    def _():
        key_blk = lax.broadcasted_iota(jnp.int32, (seq_len, aug - HEAD_DIM), 0) // SEL_LEN
        col = lax.broadcasted_iota(jnp.int32, (seq_len, aug - HEAD_DIM), 1)
        ka_ref[:, 0:HEAD_DIM] = ks_ref[0, 0]
        ka_ref[:, HEAD_DIM:] = jnp.where(key_blk == col, 1.0, 0.0).astype(BF16)
        pcol = lax.broadcasted_iota(jnp.int32, (WINDOW, aug), 1)
        kwa_ref[0:WINDOW, :] = jnp.where(pcol == FLAG_COL, 1.0, 0.0).astype(BF16)
        kwa_ref[WINDOW:, 0:HEAD_DIM] = kw_ref[0, 0]
        kwa_ref[WINDOW:, HEAD_DIM:] = jnp.zeros((seq_len, aug - HEAD_DIM), BF16)

    ones_rows = jnp.where(lax.broadcasted_iota(jnp.int32, (V_ROWS - HEAD_DIM, vst_ref.shape[4]), 0) == 0,
                          1.0, 0.0).astype(BF16)

    def v_tile(vt_ref, idx):
        return jnp.concatenate([vt_ref[0, 0, idx], ones_rows], axis=0)

    q = q_ref[...]
    qh = jnp.concatenate([q[:, h * HEAD_DIM:(h + 1) * HEAD_DIM] for h in range(HPG)], axis=0)
    flag = jnp.where(lax.broadcasted_iota(jnp.int32, (tq, aug - FLAG_COL), 1) == 0, NEG, 0.0).astype(BF16)
    qa_ref[:, 0:HEAD_DIM] = qh
    qw_ref[:, 0:HEAD_DIM] = qh
    qw_ref[:, HEAD_DIM:FLAG_COL] = jnp.zeros((rows, FLAG_COL - HEAD_DIM), BF16)
    for h in range(HPG):
        qa_ref[h * tq:(h + 1) * tq, FLAG_COL:] = flag
        qw_ref[h * tq:(h + 1) * tq, FLAG_COL:] = flag

    tk = vst_ref.shape[4]
    n_win = WINDOW // MASK_BLOCK

    def window_kind(kk, j):
        d = kk - j
        return "gt" if d == 0 else "le" if d == n_win else "full" if 0 < d < n_win else None

    sw = _dot_nt(kwa_ref[pl.ds(pl.multiple_of(t0, tq), WINDOW + tq), :], qw_ref[...])
    _, pw = _blocked_probs(sw, tq, window_kind)
    o_w = jnp.zeros((V_ROWS, rows), F32)
    for dd in range((WINDOW + tq) // tk):
        vidx = jnp.maximum(qi * (tq // tk) + dd - WINDOW // tk, 0)
        o_w = o_w + jnp.dot(v_tile(vwt_ref, vidx), pw[dd * tk:(dd + 1) * tk], preferred_element_type=F32)
    ot_ref[2] = o_w[:HEAD_DIM] * (1.0 / o_w[HEAD_DIM:HEAD_DIM + 1])

    gates = _sigmoid(gt_ref[...].astype(F32))
    for br, z_ref in enumerate((z0_ref, z1_ref, z2_ref)):
        for h in range(HPG):
            cs = slice(h * HEAD_DIM, (h + 1) * HEAD_DIM)
            z = z_ref[:, cs].astype(F32)
            c = br * HPG + h
            gz_ref[br, :, cs] = gates[:, c:c + 1] * (z * _sigmoid(z))

    ncp = kc_ref.shape[3]
    st = _dot_nt(kc_ref[0, 0, 0], qh)
    n_idx = lax.broadcasted_iota(jnp.int32, (ncp, rows), 0)
    tpos = t0 + lax.broadcasted_iota(jnp.int32, (ncp, rows), 1) % tq
    cmask = (n_idx * CMP_STRIDE + (CMP_LEN - 1) <= tpos) & (n_idx < n_cmp)
    st = jnp.where(cmask, st, NEG)
    e = jnp.where(cmask, jnp.exp(st - jnp.max(st, axis=0, keepdims=True)), 0.0)
    l = jnp.sum(e, axis=0, keepdims=True)
    p = e * (1.0 / jnp.where(l > 0.0, l, 1.0))
    ot_ref[0] = jnp.dot(vct_ref[0, 0], p.astype(BF16), preferred_element_type=F32)
    p_cmp = p[:, 0:tq]
    for h in range(1, HPG):
        p_cmp = p_cmp + p[:, h * tq:(h + 1) * tq]

    p_hi = p_cmp.astype(BF16)
    r1 = p_cmp - p_hi.astype(F32)
    p_mid = r1.astype(BF16)
    p_lo = (r1 - p_mid.astype(F32)).astype(BF16)
    ovt = ovt_ref[...]
    p_slc = (jnp.dot(ovt, p_hi, preferred_element_type=F32) + jnp.dot(ovt, p_mid, preferred_element_type=F32)
             + jnp.dot(ovt, p_lo, preferred_element_type=F32))
    blk = lax.broadcasted_iota(jnp.int32, (nsp, tq), 0)
    cur = (t0 + lax.broadcasted_iota(jnp.int32, (nsp, tq), 1)) // SEL_LEN
    valid = (blk <= cur) & (blk < n_slc)
    forced = (blk == 0) | (blk == cur) | (blk == cur - 1)
    score = jnp.where(valid, p_slc + jnp.where(forced, SEL_BONUS, 0.0), -SEL_BONUS)
    nv = nsp // SUBLANES
    sc = [score[k * SUBLANES:(k + 1) * SUBLANES] for k in range(nv)]
    sub = lax.broadcasted_iota(jnp.int32, (SUBLANES, tq), 0)
    rank = [jnp.zeros((SUBLANES, tq), jnp.int32) for _ in range(nv)]
    for i in range(n_slc):
        si = jnp.broadcast_to(score[i:i + 1, :], (SUBLANES, tq))
        for k in range(nv):
            if k * SUBLANES > i:
                beats = si >= sc[k]
            elif k * SUBLANES + SUBLANES - 1 <= i:
                beats = si > sc[k]
            else:
                beats = (si > sc[k]) | ((si == sc[k]) & (sub > i - k * SUBLANES))
            rank[k] = rank[k] + jnp.where(beats, 1, 0)
    rank = jnp.concatenate(rank, axis=0)
    sel = (rank < topk) & (score > -0.5 * SEL_BONUS)
    selb = jnp.where(sel, 0.0, NEG).T.astype(BF16)
    for h in range(HPG):
        qa_ref[h * tq:(h + 1) * tq, HEAD_DIM:FLAG_COL] = selb


    def scores(k_tile):
        return _dot_nt(k_tile, qa_ref[...])

    def online_update(s, vt_tile):
        m_old = m_ref[...]
        m_new = jnp.maximum(m_old, jnp.max(s, axis=0, keepdims=True))
        alpha = jnp.exp(m_old - m_new)
        pt = jnp.exp((s - m_new).astype(BF16))
        acc_ref[...] = alpha * acc_ref[...] + jnp.dot(vt_tile, pt, preferred_element_type=F32)
        m_ref[...] = m_new

    m_ref[...] = jnp.full(m_ref.shape, NEG, F32)
    acc_ref[...] = jnp.zeros(acc_ref.shape, F32)
    n_full = t0 // tk
    s_ref[...] = scores(ka_ref[pl.ds(0, tk), :])

    def sel_body(kt, carry):
        k1 = pl.multiple_of((kt + 1) * tk, tk)
        s_next = scores(ka_ref[pl.ds(k1, tk), :])
        online_update(s_ref[...], v_tile(vst_ref, kt))
        s_ref[...] = s_next
        return carry

    lax.fori_loop(0, n_full, sel_body, 0)

    def diagonal_kind(kk, j):
        return "full" if kk < j else "le" if kk == j else None

    m_old = m_ref[...]
    m_new, pt = _blocked_probs(s_ref[...], tq, diagonal_kind, m_old)
    acc = jnp.exp(m_old - m_new) * acc_ref[...] + jnp.dot(v_tile(vst_ref, n_full), pt, preferred_element_type=F32)
    ot_ref[1] = acc[:HEAD_DIM] * (1.0 / acc[HEAD_DIM:HEAD_DIM + 1])

    for h in range(HPG):
        cs = slice(h * HEAD_DIM, (h + 1) * HEAD_DIM)
        tot = None
        for br in range(N_BRANCH):
            term = gz_ref[br, :, cs] * ot_ref[br, :, h * tq:(h + 1) * tq].T
            tot = term if tot is None else tot + term
        o_ref[:, cs] = tot.astype(o_ref.dtype)


def _nsa_attention_t(proj, kcv, vct, kslab, vt, ovt, batch, seq_len, *, tq):
    m = proj.shape[0]
    gw = HPG * HEAD_DIM
    qt = seq_len // tq
    n_cmp = (seq_len - CMP_LEN) // CMP_STRIDE + 1
    n_slc = seq_len // SEL_LEN
    z_base = N_KV
    gate_base = (N_KV + N_BRANCH * N_KV) * (gw // LANES)
    ncp = kcv.shape[3]
    aug = 2 * HEAD_DIM
    rows = HPG * tq
    tk = vt.shape[-1]

    def zspec(br):
        return pl.BlockSpec((tq, gw), lambda b, g, i: (b * qt + i, z_base + br * N_KV + g))

    def per_group(arr, kind):
        return pl.BlockSpec((1, 1) + arr.shape[2:], lambda b, g, i: (kind * N_KV + g, b) + (0,) * (arr.ndim - 2))

    return pl.pallas_call(
        functools.partial(_nsa_t_kernel, tq=tq, n_cmp=n_cmp, n_slc=n_slc, topk=min(SEL_TOPK, n_slc)),
        grid=(batch, N_KV, qt),
        in_specs=[
            pl.BlockSpec((tq, gw), lambda b, g, i: (b * qt + i, g)),
            zspec(0), zspec(1), zspec(2),
            pl.BlockSpec((tq, LANES), lambda b, g, i: (b * qt + i, gate_base + g)),
            pl.BlockSpec((1, 1, 1, ncp, HEAD_DIM), lambda b, g, i: (0, b, g, 0, 0)),
            pl.BlockSpec((1, 1, HEAD_DIM, ncp), lambda b, g, i: (b, g, 0, 0)),
            per_group(kslab, KV_KSEL), per_group(vt, VT_SEL), per_group(kslab, KV_KWIN), per_group(vt, VT_WIN),
            pl.BlockSpec(ovt.shape, lambda b, g, i: (0, 0)),
        ],
        out_specs=pl.BlockSpec((tq, gw), lambda b, g, i: (b * qt + i, g)),
        out_shape=jax.ShapeDtypeStruct((m, N_KV * gw), BF16),
        scratch_shapes=[
            pltpu.VMEM((seq_len, aug), BF16),
            pltpu.VMEM((seq_len + WINDOW, aug), BF16),
            pltpu.VMEM((rows, aug), BF16),
            pltpu.VMEM((rows, aug), BF16),
            pltpu.VMEM((1, rows), F32),
            pltpu.VMEM((V_ROWS, rows), F32),
            pltpu.VMEM((N_BRANCH, HEAD_DIM, rows), F32),
            pltpu.VMEM((tk, rows), F32),
            pltpu.VMEM((N_BRANCH, tq, gw), F32),
        ],
        compiler_params=_cparams(("parallel", "parallel", "arbitrary")),
        name="nsa_attention",
    )(proj, proj, proj, proj, proj, kcv, vct, kslab, vt, kslab, vt, ovt)


def _selection_overlap_t(n_cmp, ncp, n_slc, nsp):
    c0 = np.arange(ncp)[None, :] * CMP_STRIDE
    s0 = np.arange(nsp)[:, None] * SEL_LEN
    ov = np.clip(np.minimum(c0 + CMP_LEN, s0 + SEL_LEN) - np.maximum(c0, s0), 0, None) / CMP_STRIDE
    ov = ov * (np.arange(ncp)[None, :] < n_cmp) * (np.arange(nsp)[:, None] < n_slc)
    return jnp.asarray(ov, dtype=BF16)


def _s5_layer(x2d, mods, norm_g, w_in, layer, s5p, d_skip, w_glu, b_glu, w_out, batch, seq_len, final_g,
              final_norm):
    d = x2d.shape[1]
    shift, scale, gate = (mods[:, None, i * d:(i + 1) * d] for i in range(3))
    e = w_glu.shape[1]
    uz = _normmod_matmul(x2d, norm_g[None], shift, scale, w_in, layer, jnp.ones((1, 2 * e), F32),
                         seq_len, tm=PROJ_ROWS, tn=1024)
    wb, wc, a_re, a_im = s5p
    y = _s5_scan(uz, wb, wc, a_re, a_im, d_skip[None].astype(F32), batch, seq_len, steps=256)
    return _glu_out_residual(y, uz, w_glu, b_glu[None].astype(F32), w_out, layer, x2d, gate,
                             final_g[None], seq_len, tm=512, final_norm=final_norm)


def _qg_weight(w_qg):
    nl, d, _ = w_qg.shape
    att = N_HEADS * HEAD_DIM
    g_end = att + N_BRANCH * N_HEADS
    w_qg = w_qg.astype(BF16)
    wg = w_qg[:, :, att:g_end].reshape(nl, d, N_BRANCH, N_KV, HPG)
    wg = jnp.transpose(wg, (0, 1, 3, 2, 4)).reshape(nl, d, N_KV, N_BRANCH * HPG)
    wg = jnp.pad(wg, ((0, 0), (0, 0), (0, 0), (0, LANES - N_BRANCH * HPG))).reshape(nl, d, N_KV * LANES)
    w = jnp.concatenate([w_qg[:, :, :att], w_qg[:, :, g_end:], wg], axis=2)
    cs = jnp.concatenate([jnp.full((att,), HEAD_DIM ** -0.5, F32), jnp.ones((w.shape[2] - att,), F32)])
    return w, cs[None]


def kernel(x, c, norm_g, mod_w, mod_b, ssm_w_in, ssm_lam_re, ssm_lam_im, ssm_log_step, ssm_b_re, ssm_b_im, ssm_c_re, ssm_c_im, ssm_d, ssm_w_glu, ssm_b_glu, ssm_w_out, kv_norm_g, kv_mod_w, kv_mod_b, w_kv, cmp_pe, cmp_w1, cmp_b1, cmp_w2, cmp_b2, nsa_w_qg, nsa_w_o, final_norm_g):
    batch, seq_len, d = x.shape
    depth = mod_w.shape[0]
    n_a = ssm_w_in.shape[0]
    m = batch * seq_len
    x2d = x.reshape(m, d)

    c_pad = jnp.pad(c, ((0, 2 * SUBLANES - batch), (0, 0)))
    mods = _cond_matmul(c_pad, mod_w, mod_b[:, None])[:, :batch]
    kv_mods = _cond_matmul(c_pad, kv_mod_w[None], kv_mod_b[None, None])[0, :batch]

    w_in, w_glu, w_out = ssm_w_in.astype(BF16), ssm_w_glu.astype(BF16), ssm_w_out.astype(BF16)
    for layer in range(n_a):
        s5p = _s5_params(ssm_lam_re[layer], ssm_lam_im[layer], ssm_log_step[layer], ssm_b_re[layer],
                         ssm_b_im[layer], ssm_c_re[layer], ssm_c_im[layer])
        x2d = _s5_layer(x2d, mods[layer], norm_g[layer], w_in, layer, s5p, ssm_d[layer], w_glu,
                        ssm_b_glu[layer], w_out, batch, seq_len, final_norm_g,
                        final_norm=(layer == depth - 1))

    tq = KEY_TILE
    assert seq_len % KEY_TILE == 0 and WINDOW % KEY_TILE == 0 and WINDOW % MASK_BLOCK == 0
    kv_shift, kv_scale = kv_mods[:, None, :d], kv_mods[:, None, d:]
    kslab, vt = _kv_projection(x2d, kv_norm_g[None], kv_shift, kv_scale, w_kv.astype(BF16), batch, seq_len,
                               tm=PROJ_ROWS, tk=KEY_TILE)
    kslab = kslab.reshape(4 * N_KV, batch, seq_len, HEAD_DIM)

    n_cmp = (seq_len - CMP_LEN) // CMP_STRIDE + 1
    n_slc = seq_len // SEL_LEN
    nb16 = seq_len // CMP_STRIDE
    half = CMP_STRIDE * HEAD_DIM
    kv16 = kslab[:2 * N_KV].reshape(2 * N_KV, batch, nb16, half)
    w1cat = jnp.concatenate([cmp_w1[:, :half], cmp_w1[:, half:]], axis=2).astype(BF16)
    pe2 = jnp.pad(cmp_pe.reshape(2, 2, half), ((0, 0), (0, 2 * SUBLANES - 2), (0, 0)))
    kcv = _compress(kv16, w1cat, pe2, cmp_b1[:, None], cmp_w2.astype(BF16), cmp_b2[:, None], batch, n_cmp)

    nsp = FLAG_COL - HEAD_DIM
    assert n_slc <= nsp
    vct = jnp.swapaxes(kcv[1], -1, -2)
    ovt = _selection_overlap_t(n_cmp, nb16, n_slc, nsp)

    wq, cs = _qg_weight(nsa_w_qg)
    w_o = nsa_w_o.astype(BF16)
    for layer in range(n_a, depth):
        j = layer - n_a
        shift, scale, gate = (mods[layer][:, None, i * d:(i + 1) * d] for i in range(3))
        proj = _normmod_matmul(x2d, norm_g[layer][None], shift, scale, wq, j, cs, seq_len, tm=PROJ_ROWS, tn=512)
        o = _nsa_attention_t(proj, kcv, vct, kslab, vt, ovt, batch, seq_len, tq=tq)
        x2d = _mm_residual(o, w_o, j, x2d, gate, final_norm_g[None], seq_len, tm=512,
                           final_norm=(layer == depth - 1))

    return x2d.reshape(batch, seq_len, d)
```

```python
import functools
import math

import jax
import jax.numpy as jnp
import numpy as np
from jax import lax
from jax.experimental import pallas as pl
from jax.experimental.pallas import tpu as pltpu

F32 = jnp.float32
BF16 = jnp.bfloat16

SSM_GROUP = 16
SSM_STATE = 64
N_HEADS = 16
N_KV = 4
HPG = N_HEADS // N_KV
HEAD_DIM = 128
N_BRANCH = 3
CMP_LEN = 32
CMP_STRIDE = 16
SEL_LEN = 64
SEL_TOPK = 16
WINDOW = 512
SEL_BONUS = 1e3
NEG = -1e30
EPS = 1e-6

LANES = 128
SUBLANES = 8
MXU_DIM = 256
VMEM_LIMIT = 56 * 1024 * 1024

PROJ_ROWS = 1024

GROUPS_PER_BLOCK = MXU_DIM // SSM_GROUP
STATES_PER_BLOCK = GROUPS_PER_BLOCK * SSM_STATE


def _cparams(sem):
    return pltpu.CompilerParams(dimension_semantics=sem, vmem_limit_bytes=VMEM_LIMIT)


def _gelu_tanh(x):
    return x * (0.5 * (1.0 + jnp.tanh(math.sqrt(2.0 / math.pi) * (x + 0.044715 * (x * x * x)))))


def _sigmoid(x):
    return 0.5 * jnp.tanh(0.5 * x) + 0.5


def _dot_nt(a, b):
    return lax.dot_general(a, b, (((1,), (1,)), ((), ())), preferred_element_type=F32)


def _cond_kernel(c_ref, w_ref, b_ref, o_ref):
    c = c_ref[...]
    ca = (c * _sigmoid(c)).astype(BF16)
    acc = jnp.dot(ca, w_ref[0].astype(BF16), preferred_element_type=F32)
    o_ref[0] = acc + b_ref[0]


def _cond_matmul(c_pad, w, b, tn=512):
    nl, d, n = w.shape
    r = c_pad.shape[0]
    return pl.pallas_call(
        _cond_kernel,
        grid=(nl, n // tn),
        in_specs=[
            pl.BlockSpec((r, d), lambda l, j: (0, 0)),
            pl.BlockSpec((1, d, tn), lambda l, j: (l, 0, j)),
            pl.BlockSpec((1, 1, tn), lambda l, j: (l, 0, j)),
        ],
        out_specs=pl.BlockSpec((1, r, tn), lambda l, j: (l, 0, j)),
        out_shape=jax.ShapeDtypeStruct((nl, r, n), F32),
        compiler_params=_cparams(("parallel", "parallel")),
        name="cond_matmul",
    )(c_pad, w, b)


NORM_SLAB = 16


def _norm_modulate(x_ref, g_ref, sh_ref, sc_ref, h_ref):
    gain = g_ref[...] * (1.0 + sc_ref[0])
    shift = sh_ref[0]

    def slab(r, carry):
        rs = pl.ds(pl.multiple_of(r * NORM_SLAB, NORM_SLAB), NORM_SLAB)
        x = x_ref[rs, :]
        y = x * lax.rsqrt(jnp.mean(x * x, axis=-1, keepdims=True) + EPS)
        h_ref[rs, :] = (y * gain + shift).astype(BF16)
        return carry

    lax.fori_loop(0, x_ref.shape[0] // NORM_SLAB, slab, 0, unroll=8)


def _normmod_mm_kernel(x_ref, g_ref, sh_ref, sc_ref, w_ref, cs_ref, o_ref, h_ref):
    @pl.when(pl.program_id(1) == 0)
    def _():
        _norm_modulate(x_ref, g_ref, sh_ref, sc_ref, h_ref)

    acc = jnp.dot(h_ref[...], w_ref[...], preferred_element_type=F32) * cs_ref[...]
    o_ref[...] = acc.astype(o_ref.dtype)


def _normmod_matmul(x2d, g, shift, scale, w, layer, colscale, seq_len, *, tm, tn):
    m, d = x2d.shape
    n = w.shape[2]
    rows_per_batch = seq_len // tm
    return pl.pallas_call(
        _normmod_mm_kernel,
        grid=(m // tm, n // tn),
        in_specs=[
            pl.BlockSpec((tm, d), lambda i, j: (i, 0)),
            pl.BlockSpec((1, d), lambda i, j: (0, 0)),
            pl.BlockSpec((1, 1, d), lambda i, j: (i // rows_per_batch, 0, 0)),
            pl.BlockSpec((1, 1, d), lambda i, j: (i // rows_per_batch, 0, 0)),
            pl.BlockSpec((None, d, tn), lambda i, j: (layer, 0, j)),
            pl.BlockSpec((1, tn), lambda i, j: (0, j)),
        ],
        out_specs=pl.BlockSpec((tm, tn), lambda i, j: (i, j)),
        out_shape=jax.ShapeDtypeStruct((m, n), BF16),
        scratch_shapes=[pltpu.VMEM((tm, d), BF16)],
        compiler_params=_cparams(("parallel", "arbitrary")),
        name="normmod_matmul",
    )(x2d, g, shift, scale, w, colscale)


KV_KCMP, KV_VCMP, KV_KSEL, KV_KWIN = range(4)
VT_SEL, VT_WIN = range(2)
_SEL_V_COL, _WIN_V_COL = 3, 5


def _kv_proj_kernel(x_ref, g_ref, sh_ref, sc_ref, w_ref, ks_ref, vt_ref, h_ref):
    j = pl.program_id(1)

    @pl.when(j == 0)
    def _():
        _norm_modulate(x_ref, g_ref, sh_ref, sc_ref, h_ref)

    acc = jnp.dot(h_ref[...], w_ref[...], preferred_element_type=F32)
    tk = vt_ref.shape[-1]
    is_value = (j == _SEL_V_COL) | (j == _WIN_V_COL)

    @pl.when(jnp.logical_not(is_value))
    def _():
        for s in range(N_KV):
            ks_ref[s] = acc[:, s * HEAD_DIM:(s + 1) * HEAD_DIM].astype(ks_ref.dtype)

    @pl.when(is_value)
    def _():
        for s in range(N_KV):
            for r in range(acc.shape[0] // tk):
                blk = acc[r * tk:(r + 1) * tk, s * HEAD_DIM:(s + 1) * HEAD_DIM]
                vt_ref[s, 0, r] = blk.T.astype(vt_ref.dtype)


def _kv_projection(x2d, g, shift, scale, w, batch, seq_len, *, tm, tk):
    m, d = x2d.shape
    tn = N_KV * HEAD_DIM
    assert w.shape[1] == 6 * tn
    rpb = seq_len // tm

    def slab_kind(j):
        return j - (j >= _SEL_V_COL).astype(jnp.int32) - (j >= _WIN_V_COL).astype(jnp.int32)

    return pl.pallas_call(
        _kv_proj_kernel,
        grid=(m // tm, 6),
        in_specs=[
            pl.BlockSpec((tm, d), lambda i, j: (i, 0)),
            pl.BlockSpec((1, d), lambda i, j: (0, 0)),
            pl.BlockSpec((1, 1, d), lambda i, j: (i // rpb, 0, 0)),
            pl.BlockSpec((1, 1, d), lambda i, j: (i // rpb, 0, 0)),
            pl.BlockSpec((d, tn), lambda i, j: (0, j)),
        ],
        out_specs=[
            pl.BlockSpec((N_KV, tm, HEAD_DIM), lambda i, j: (slab_kind(j), i, 0)),
            pl.BlockSpec((N_KV, 1, tm // tk, HEAD_DIM, tk),
                         lambda i, j: ((j > _SEL_V_COL).astype(jnp.int32), i // rpb, i % rpb, 0, 0)),
        ],
        out_shape=[
            jax.ShapeDtypeStruct((4 * N_KV, m, HEAD_DIM), BF16),
            jax.ShapeDtypeStruct((2 * N_KV, batch, seq_len // tk, HEAD_DIM, tk), BF16),
        ],
        scratch_shapes=[pltpu.VMEM((tm, d), BF16)],
        compiler_params=_cparams(("parallel", "arbitrary")),
        name="kv_projection",
    )(x2d, g, shift, scale, w)


def _mm_res_kernel(a_ref, w_ref, x_ref, gate_ref, fg_ref, o_ref, *, final_norm):
    acc = jnp.dot(a_ref[...], w_ref[...], preferred_element_type=F32)
    xn = x_ref[...] + gate_ref[0] * acc
    if final_norm:
        xn = xn * lax.rsqrt(jnp.mean(xn * xn, axis=-1, keepdims=True) + EPS) * fg_ref[...]
    o_ref[...] = xn


def _mm_residual(a, w, layer, x2d, gate, final_g, seq_len, *, tm, final_norm):
    m, k = a.shape
    n = w.shape[2]
    rows_per_batch = seq_len // tm
    return pl.pallas_call(
        functools.partial(_mm_res_kernel, final_norm=final_norm),
        grid=(m // tm,),
        in_specs=[
            pl.BlockSpec((tm, k), lambda i: (i, 0)),
            pl.BlockSpec((None, k, n), lambda i: (layer, 0, 0), pipeline_mode=pl.Buffered(1)),
            pl.BlockSpec((tm, n), lambda i: (i, 0)),
            pl.BlockSpec((1, 1, n), lambda i: (i // rows_per_batch, 0, 0)),
            pl.BlockSpec((1, n), lambda i: (0, 0)),
        ],
        out_specs=pl.BlockSpec((tm, n), lambda i: (i, 0)),
        out_shape=jax.ShapeDtypeStruct((m, n), F32),
        compiler_params=_cparams(("parallel",)),
        name="matmul_residual_final" if final_norm else "matmul_residual",
    )(a, w, x2d, gate, final_g)


def _glu_out_kernel(y_ref, z_ref, wg_ref, bg_ref, wo_ref, x_ref, gate_ref, fg_ref, o_ref, *, final_norm):
    y = y_ref[...]
    lin = jnp.dot(y, wg_ref[...], preferred_element_type=F32) + bg_ref[...]
    z = z_ref[...].astype(F32)
    t = ((y.astype(F32) * _sigmoid(lin)) * (z * _sigmoid(z))).astype(BF16)
    xn = x_ref[...] + gate_ref[0] * jnp.dot(t, wo_ref[...], preferred_element_type=F32)
    if final_norm:
        xn = xn * lax.rsqrt(jnp.mean(xn * xn, axis=-1, keepdims=True) + EPS) * fg_ref[...]
    o_ref[...] = xn


def _glu_out_residual(y, uz, w_glu, b_glu, w_out, layer, x2d, gate, final_g, seq_len, *, tm, final_norm):
    m, e = y.shape
    n = w_out.shape[2]
    rows_per_batch = seq_len // tm
    resident = pl.Buffered(1)
    return pl.pallas_call(
        functools.partial(_glu_out_kernel, final_norm=final_norm),
        grid=(m // tm,),
        in_specs=[
            pl.BlockSpec((tm, e), lambda i: (i, 0)),
            pl.BlockSpec((tm, e), lambda i: (i, 1)),
            pl.BlockSpec((None, e, e), lambda i: (layer, 0, 0), pipeline_mode=resident),
            pl.BlockSpec((1, e), lambda i: (0, 0)),
            pl.BlockSpec((None, e, n), lambda i: (layer, 0, 0), pipeline_mode=resident),
            pl.BlockSpec((tm, n), lambda i: (i, 0)),
            pl.BlockSpec((1, 1, n), lambda i: (i // rows_per_batch, 0, 0)),
            pl.BlockSpec((1, n), lambda i: (0, 0)),
        ],
        out_specs=pl.BlockSpec((tm, n), lambda i: (i, 0)),
        out_shape=jax.ShapeDtypeStruct((m, n), F32),
        compiler_params=_cparams(("parallel",)),
        name="s5_glu_out",
    )(y, uz, w_glu, b_glu, w_out, x2d, gate, final_g)


SCAN_ROW_PITCH = 12


def _s5_scan_kernel(u_ref, wb_ref, wc_ref, are_ref, aim_ref, d_ref, y_ref, s_ref, st_ref, *, steps, nblk):
    nslab = STATES_PER_BLOCK // LANES

    @pl.when(pl.program_id(1) == 0)
    def _():
        st_ref[...] = jnp.zeros_like(st_ref)

    for blk in range(nblk):
        bu = jnp.dot(u_ref[:, blk * MXU_DIM:(blk + 1) * MXU_DIM], wb_ref[blk], preferred_element_type=F32)
        for k in range(2 * nslab):
            s_ref[k, pl.ds(blk, steps, stride=SCAN_ROW_PITCH), :] = bu[:, k * LANES:(k + 1) * LANES]

    a_re = [are_ref[:, k * LANES:(k + 1) * LANES] for k in range(nslab)]
    a_im = [aim_ref[:, k * LANES:(k + 1) * LANES] for k in range(nslab)]

    def step(t, carry):
        xr, xi = carry
        r0 = t * SCAN_ROW_PITCH
        nr, ni = [], []
        for k in range(nslab):
            b_re = s_ref[k, pl.ds(r0, nblk), :]
            b_im = s_ref[nslab + k, pl.ds(r0, nblk), :]
            v_re = a_re[k] * xr[k] - a_im[k] * xi[k] + b_re
            v_im = a_re[k] * xi[k] + a_im[k] * xr[k] + b_im
            s_ref[k, pl.ds(r0, nblk), :] = v_re
            s_ref[nslab + k, pl.ds(r0, nblk), :] = v_im
            nr.append(v_re)
            ni.append(v_im)
        return tuple(nr), tuple(ni)

    init = (tuple(st_ref[k] for k in range(nslab)), tuple(st_ref[nslab + k] for k in range(nslab)))
    xr, xi = lax.fori_loop(0, steps, step, init, unroll=2)
    for k in range(nslab):
        st_ref[k] = xr[k]
        st_ref[nslab + k] = xi[k]

    for blk in range(nblk):
        xs = jnp.concatenate(
            [s_ref[k, pl.ds(blk, steps, stride=SCAN_ROW_PITCH), :].astype(BF16) for k in range(2 * nslab)], axis=1)
        cols = slice(blk * MXU_DIM, (blk + 1) * MXU_DIM)
        yb = jnp.dot(xs, wc_ref[blk], preferred_element_type=F32)
        yb = yb + d_ref[:, cols] * u_ref[:, cols].astype(F32)
        y_ref[:, cols] = _gelu_tanh(yb).astype(y_ref.dtype)


def _s5_scan(uz, wb, wc, a_re, a_im, d_skip, batch, seq_len, *, steps):
    m = uz.shape[0]
    nblk, _, two_states = wb.shape
    e = nblk * MXU_DIM
    chunks = seq_len // steps
    nslab2 = two_states // LANES
    return pl.pallas_call(
        functools.partial(_s5_scan_kernel, steps=steps, nblk=nblk),
        grid=(batch, chunks),
        in_specs=[
            pl.BlockSpec((steps, e), lambda b, c: (b * chunks + c, 0)),
            pl.BlockSpec(wb.shape, lambda b, c: (0, 0, 0), pipeline_mode=pl.Buffered(1)),
            pl.BlockSpec(wc.shape, lambda b, c: (0, 0, 0), pipeline_mode=pl.Buffered(1)),
            pl.BlockSpec(a_re.shape, lambda b, c: (0, 0)),
            pl.BlockSpec(a_im.shape, lambda b, c: (0, 0)),
            pl.BlockSpec((1, e), lambda b, c: (0, 0)),
        ],
        out_specs=pl.BlockSpec((steps, e), lambda b, c: (b * chunks + c, 0)),
        out_shape=jax.ShapeDtypeStruct((m, e), BF16),
        scratch_shapes=[
            pltpu.VMEM((nslab2, steps * SCAN_ROW_PITCH, LANES), F32),
            pltpu.VMEM((nslab2, nblk, LANES), F32),
        ],
        compiler_params=_cparams(("parallel", "arbitrary")),
        name="s5_scan",
    )(uz, wb, wc, a_re, a_im, d_skip)


def _s5_params(lam_re, lam_im, log_step, b_re, b_im, c_re, c_im):
    g, n = lam_re.shape
    dt = jnp.exp(log_step.astype(F32))[:, None]
    lr, li = lam_re.astype(F32), lam_im.astype(F32)
    mag = jnp.exp(lr * dt)
    a_re, a_im = mag * jnp.cos(li * dt), mag * jnp.sin(li * dt)
    den = lr * lr + li * li
    coef_re = ((a_re - 1.0) * lr + a_im * li) / den
    coef_im = (a_im * lr - (a_re - 1.0) * li) / den
    br, bi = b_re.astype(F32), b_im.astype(F32)
    bb_re = coef_re[..., None] * br - coef_im[..., None] * bi
    bb_im = coef_re[..., None] * bi + coef_im[..., None] * br
    nblk = g // GROUPS_PER_BLOCK

    def blockdiag(t, inner):
        rows = t.shape[1]
        cols = GROUPS_PER_BLOCK * inner
        expand = (jnp.arange(cols)[None, :] % inner == jnp.arange(inner)[:, None]).astype(F32)
        tiled = jnp.einsum('brn,nk->brk', t, expand, precision=lax.Precision.HIGHEST)
        same_group = (jnp.arange(rows)[:, None] // (rows // GROUPS_PER_BLOCK)
                      == jnp.arange(cols)[None, :] // inner)
        return jnp.where(same_group, tiled, 0.0)

    def blockdiag_in(bb):
        t = jnp.swapaxes(bb.reshape(nblk, GROUPS_PER_BLOCK, n, SSM_GROUP), 2, 3)
        return blockdiag(t.reshape(nblk, GROUPS_PER_BLOCK * SSM_GROUP, n), n)

    def blockdiag_out(cc):
        t = jnp.swapaxes(cc.astype(F32).reshape(nblk, GROUPS_PER_BLOCK, SSM_GROUP, n), 2, 3)
        return blockdiag(t.reshape(nblk, GROUPS_PER_BLOCK * n, SSM_GROUP), SSM_GROUP)

    wb = jnp.concatenate([blockdiag_in(bb_re), blockdiag_in(bb_im)], axis=2).astype(BF16)
    wc = jnp.concatenate([blockdiag_out(c_re), -blockdiag_out(c_im)], axis=1).astype(BF16)
    return wb, wc, a_re.reshape(nblk, GROUPS_PER_BLOCK * n), a_im.reshape(nblk, GROUPS_PER_BLOCK * n)


def _compress_kernel(x_ref, w1_ref, pe_ref, b1_ref, w2_ref, b2_ref, o_ref, sh_ref, *, n_cmp):
    nb = x_ref.shape[2]
    w1 = w1_ref[0]
    r = jnp.dot(x_ref[0, 0], w1, preferred_element_type=F32)
    pe = pe_ref[0]
    pe_hi = pe.astype(BF16)
    pe_lo = (pe - pe_hi.astype(F32)).astype(BF16)
    pw = (jnp.dot(pe_hi, w1, preferred_element_type=F32) + jnp.dot(pe_lo, w1, preferred_element_type=F32))
    const = pw[0:1, :HEAD_DIM] + pw[1:2, HEAD_DIM:] + b1_ref[0]
    sh_ref[pl.ds(0, nb), :] = r[:, HEAD_DIM:]
    sh_ref[pl.ds(nb, SUBLANES), :] = jnp.zeros((SUBLANES, HEAD_DIM), F32)
    hid = r[:, :HEAD_DIM] + sh_ref[pl.ds(1, nb), :] + const
    out = jnp.dot(_gelu_tanh(hid).astype(BF16), w2_ref[0], preferred_element_type=F32) + b2_ref[0]
    row = lax.broadcasted_iota(jnp.int32, out.shape, 0)
    o_ref[0, 0, 0] = jnp.where(row < n_cmp, out, 0.0).astype(o_ref.dtype)


def _compress(kv16, w1cat, pe2, b1, w2, b2, batch, n_cmp):
    nb, wide = kv16.shape[2], kv16.shape[3]
    return pl.pallas_call(
        functools.partial(_compress_kernel, n_cmp=n_cmp),
        grid=(2, N_KV, batch),
        in_specs=[
            pl.BlockSpec((1, 1, nb, wide), lambda s, g, b: (s * N_KV + g, b, 0, 0)),
            pl.BlockSpec((1, wide, 2 * HEAD_DIM), lambda s, g, b: (s, 0, 0)),
            pl.BlockSpec((1, 2 * SUBLANES, wide), lambda s, g, b: (s, 0, 0)),
            pl.BlockSpec((1, 1, HEAD_DIM), lambda s, g, b: (s, 0, 0)),
            pl.BlockSpec((1, HEAD_DIM, HEAD_DIM), lambda s, g, b: (s, 0, 0)),
            pl.BlockSpec((1, 1, HEAD_DIM), lambda s, g, b: (s, 0, 0)),
        ],
        out_specs=pl.BlockSpec((1, 1, 1, nb, HEAD_DIM), lambda s, g, b: (s, b, g, 0, 0)),
        out_shape=jax.ShapeDtypeStruct((2, batch, N_KV, nb, HEAD_DIM), BF16),
        scratch_shapes=[pltpu.VMEM((nb + SUBLANES, HEAD_DIM), F32)],
        compiler_params=_cparams(("parallel", "parallel", "parallel")),
        name="nsa_compress",
    )(kv16, w1cat, pe2, b1, w2, b2)


FLAG_COL = HEAD_DIM + HEAD_DIM // 2
V_ROWS = HEAD_DIM + 16
KEY_TILE = 512
MASK_BLOCK = 128


def _blocked_probs(s, tq, kind_of, m_prev=None):
    nkb, nlt, nj = s.shape[0] // MASK_BLOCK, s.shape[1] // MASK_BLOCK, tq // MASK_BLOCK
    krow = lax.broadcasted_iota(jnp.int32, (MASK_BLOCK, MASK_BLOCK), 0)
    tcol = lax.broadcasted_iota(jnp.int32, (MASK_BLOCK, MASK_BLOCK), 1)
    visible = {"le": krow <= tcol, "gt": krow > tcol}
    zero = jnp.zeros((MASK_BLOCK, MASK_BLOCK), BF16)
    ms, cols = [], []
    for c in range(nlt):
        ls = slice(c * MASK_BLOCK, (c + 1) * MASK_BLOCK)
        blks = {}
        for kk in range(nkb):
            kind = kind_of(kk, c % nj)
            if kind is not None:
                b = s[kk * MASK_BLOCK:(kk + 1) * MASK_BLOCK, ls]
                blks[kk] = b if kind == "full" else jnp.where(visible[kind], b, NEG)
        m_c = jnp.max(functools.reduce(jnp.maximum, blks.values()), axis=0, keepdims=True)
        if m_prev is not None:
            m_c = jnp.maximum(m_c, m_prev[:, ls])
        ms.append(m_c)
        cols.append(jnp.concatenate(
            [jnp.exp((blks[kk] - m_c).astype(BF16)) if kk in blks else zero for kk in range(nkb)], axis=0))
    return jnp.concatenate(ms, axis=1), jnp.concatenate(cols, axis=1)


def _nsa_t_kernel(q_ref, z0_ref, z1_ref, z2_ref, gt_ref, kc_ref, vct_ref, ks_ref, vst_ref, kw_ref, vwt_ref,
                  ovt_ref, o_ref, ka_ref, kwa_ref, qa_ref, qw_ref, m_ref, acc_ref, ot_ref, s_ref, gz_ref,
                  *, tq, n_cmp, n_slc, topk):
    qi = pl.program_id(2)
    t0 = qi * tq
    rows = HPG * tq
    nsp = ovt_ref.shape[0]
    seq_len, aug = ka_ref.shape

    @pl.when(qi == 0)
    def _():
        key_blk = lax.broadcasted_iota(jnp.int32, (seq_len, aug - HEAD_DIM), 0) // SEL_LEN
        col = lax.broadcasted_iota(jnp.int32, (seq_len, aug - HEAD_DIM), 1)
        ka_ref[:, 0:HEAD_DIM] = ks_ref[0, 0]
        ka_ref[:, HEAD_DIM:] = jnp.where(key_blk == col, 1.0, 0.0).astype(BF16)
        pcol = lax.broadcasted_iota(jnp.int32, (WINDOW, aug), 1)
        kwa_ref[0:WINDOW, :] = jnp.where(pcol == FLAG_COL, 1.0, 0.0).astype(BF16)
        kwa_ref[WINDOW:, 0:HEAD_DIM] = kw_ref[0, 0]
        kwa_ref[WINDOW:, HEAD_DIM:] = jnp.zeros((seq_len, aug - HEAD_DIM), BF16)

    ones_rows = jnp.where(lax.broadcasted_iota(jnp.int32, (V_ROWS - HEAD_DIM, vst_ref.shape[4]), 0) == 0,
                          1.0, 0.0).astype(BF16)

    def v_tile(vt_ref, idx):
        return jnp.concatenate([vt_ref[0, 0, idx], ones_rows], axis=0)

    q = q_ref[...]
    qh = jnp.concatenate([q[:, h * HEAD_DIM:(h + 1) * HEAD_DIM] for h in range(HPG)], axis=0)
    flag = jnp.where(lax.broadcasted_iota(jnp.int32, (tq, aug - FLAG_COL), 1) == 0, NEG, 0.0).astype(BF16)
    qa_ref[:, 0:HEAD_DIM] = qh
    qw_ref[:, 0:HEAD_DIM] = qh
    qw_ref[:, HEAD_DIM:FLAG_COL] = jnp.zeros((rows, FLAG_COL - HEAD_DIM), BF16)
    for h in range(HPG):
        qa_ref[h * tq:(h + 1) * tq, FLAG_COL:] = flag
        qw_ref[h * tq:(h + 1) * tq, FLAG_COL:] = flag

    tk = vst_ref.shape[4]
    n_win = WINDOW // MASK_BLOCK

    def window_kind(kk, j):
        d = kk - j
        return "gt" if d == 0 else "le" if d == n_win else "full" if 0 < d < n_win else None

    sw = _dot_nt(kwa_ref[pl.ds(pl.multiple_of(t0, tq), WINDOW + tq), :], qw_ref[...])
    _, pw = _blocked_probs(sw, tq, window_kind)
    o_w = jnp.zeros((V_ROWS, rows), F32)
    for dd in range((WINDOW + tq) // tk):
        vidx = jnp.maximum(qi * (tq // tk) + dd - WINDOW // tk, 0)
        o_w = o_w + jnp.dot(v_tile(vwt_ref, vidx), pw[dd * tk:(dd + 1) * tk], preferred_element_type=F32)
    ot_ref[2] = o_w[:HEAD_DIM] * (1.0 / o_w[HEAD_DIM:HEAD_DIM + 1])

    gates = _sigmoid(gt_ref[...].astype(F32))
    for br, z_ref in enumerate((z0_ref, z1_ref, z2_ref)):
        for h in range(HPG):
            cs = slice(h * HEAD_DIM, (h + 1) * HEAD_DIM)
            z = z_ref[:, cs].astype(F32)
            c = br * HPG + h
            gz_ref[br, :, cs] = gates[:, c:c + 1] * (z * _sigmoid(z))

    ncp = kc_ref.shape[3]
    st = _dot_nt(kc_ref[0, 0, 0], qh)
    n_idx = lax.broadcasted_iota(jnp.int32, (ncp, rows), 0)
    tpos = t0 + lax.broadcasted_iota(jnp.int32, (ncp, rows), 1) % tq
    cmask = (n_idx * CMP_STRIDE + (CMP_LEN - 1) <= tpos) & (n_idx < n_cmp)
    st = jnp.where(cmask, st, NEG)
    e = jnp.where(cmask, jnp.exp(st - jnp.max(st, axis=0, keepdims=True)), 0.0)
    l = jnp.sum(e, axis=0, keepdims=True)
    p = e * (1.0 / jnp.where(l > 0.0, l, 1.0))
    ot_ref[0] = jnp.dot(vct_ref[0, 0], p.astype(BF16), preferred_element_type=F32)
    p_cmp = p[:, 0:tq]
    for h in range(1, HPG):
        p_cmp = p_cmp + p[:, h * tq:(h + 1) * tq]

    p_hi = p_cmp.astype(BF16)
    r1 = p_cmp - p_hi.astype(F32)
    p_mid = r1.astype(BF16)
    p_lo = (r1 - p_mid.astype(F32)).astype(BF16)
    ovt = ovt_ref[...]
    p_slc = (jnp.dot(ovt, p_hi, preferred_element_type=F32) + jnp.dot(ovt, p_mid, preferred_element_type=F32)
             + jnp.dot(ovt, p_lo, preferred_element_type=F32))
    blk = lax.broadcasted_iota(jnp.int32, (nsp, tq), 0)
    cur = (t0 + lax.broadcasted_iota(jnp.int32, (nsp, tq), 1)) // SEL_LEN
    valid = (blk <= cur) & (blk < n_slc)
    forced = (blk == 0) | (blk == cur) | (blk == cur - 1)
    score = jnp.where(valid, p_slc + jnp.where(forced, SEL_BONUS, 0.0), -SEL_BONUS)
    nv = nsp // SUBLANES
    sc = [score[k * SUBLANES:(k + 1) * SUBLANES] for k in range(nv)]
    sub = lax.broadcasted_iota(jnp.int32, (SUBLANES, tq), 0)
    rank = [jnp.zeros((SUBLANES, tq), jnp.int32) for _ in range(nv)]
    for i in range(n_slc):
        si = jnp.broadcast_to(score[i:i + 1, :], (SUBLANES, tq))
        for k in range(nv):
            if k * SUBLANES > i:
                beats = si >= sc[k]
            elif k * SUBLANES + SUBLANES - 1 <= i:
                beats = si > sc[k]
            else:
                beats = (si > sc[k]) | ((si == sc[k]) & (sub > i - k * SUBLANES))
            rank[k] = rank[k] + jnp.where(beats, 1, 0)
    rank = jnp.concatenate(rank, axis=0)
    sel = (rank < topk) & (score > -0.5 * SEL_BONUS)
    selb = jnp.where(sel, 0.0, NEG).T.astype(BF16)
    for h in range(HPG):
        qa_ref[h * tq:(h + 1) * tq, HEAD_DIM:FLAG_COL] = selb


    def scores(k_tile):
        return _dot_nt(k_tile, qa_ref[...])

    def online_update(s, vt_tile):
        m_old = m_ref[...]
        m_new = jnp.maximum(m_old, jnp.max(s, axis=0, keepdims=True))
        alpha = jnp.exp(m_old - m_new)
        pt = jnp.exp((s - m_new).astype(BF16))
        acc_ref[...] = alpha * acc_ref[...] + jnp.dot(vt_tile, pt, preferred_element_type=F32)
        m_ref[...] = m_new

    m_ref[...] = jnp.full(m_ref.shape, NEG, F32)
    acc_ref[...] = jnp.zeros(acc_ref.shape, F32)
    n_full = t0 // tk
    s_ref[...] = scores(ka_ref[pl.ds(0, tk), :])

    def sel_body(kt, carry):
        k1 = pl.multiple_of((kt + 1) * tk, tk)
        s_next = scores(ka_ref[pl.ds(k1, tk), :])
        online_update(s_ref[...], v_tile(vst_ref, kt))
        s_ref[...] = s_next
        return carry

    lax.fori_loop(0, n_full, sel_body, 0)

    def diagonal_kind(kk, j):
        return "full" if kk < j else "le" if kk == j else None

    m_old = m_ref[...]
    m_new, pt = _blocked_probs(s_ref[...], tq, diagonal_kind, m_old)
    acc = jnp.exp(m_old - m_new) * acc_ref[...] + jnp.dot(v_tile(vst_ref, n_full), pt, preferred_element_type=F32)
    ot_ref[1] = acc[:HEAD_DIM] * (1.0 / acc[HEAD_DIM:HEAD_DIM + 1])

    for h in range(HPG):
        cs = slice(h * HEAD_DIM, (h + 1) * HEAD_DIM)
        tot = None
        for br in range(N_BRANCH):
            term = gz_ref[br, :, cs] * ot_ref[br, :, h * tq:(h + 1) * tq].T
            tot = term if tot is None else tot + term
        o_ref[:, cs] = tot.astype(o_ref.dtype)


def _nsa_attention_t(proj, kcv, vct, kslab, vt, ovt, batch, seq_len, *, tq):
    m = proj.shape[0]
    gw = HPG * HEAD_DIM
    qt = seq_len // tq
    n_cmp = (seq_len - CMP_LEN) // CMP_STRIDE + 1
    n_slc = seq_len // SEL_LEN
    z_base = N_KV
    gate_base = (N_KV + N_BRANCH * N_KV) * (gw // LANES)
    ncp = kcv.shape[3]
    aug = 2 * HEAD_DIM
    rows = HPG * tq
    tk = vt.shape[-1]

    def zspec(br):
        return pl.BlockSpec((tq, gw), lambda b, g, i: (b * qt + i, z_base + br * N_KV + g))

    def per_group(arr, kind):
        return pl.BlockSpec((1, 1) + arr.shape[2:], lambda b, g, i: (kind * N_KV + g, b) + (0,) * (arr.ndim - 2))

    return pl.pallas_call(
        functools.partial(_nsa_t_kernel, tq=tq, n_cmp=n_cmp, n_slc=n_slc, topk=min(SEL_TOPK, n_slc)),
        grid=(batch, N_KV, qt),
        in_specs=[
            pl.BlockSpec((tq, gw), lambda b, g, i: (b * qt + i, g)),
            zspec(0), zspec(1), zspec(2),
            pl.BlockSpec((tq, LANES), lambda b, g, i: (b * qt + i, gate_base + g)),
            pl.BlockSpec((1, 1, 1, ncp, HEAD_DIM), lambda b, g, i: (0, b, g, 0, 0)),
            pl.BlockSpec((1, 1, HEAD_DIM, ncp), lambda b, g, i: (b, g, 0, 0)),
            per_group(kslab, KV_KSEL), per_group(vt, VT_SEL), per_group(kslab, KV_KWIN), per_group(vt, VT_WIN),
            pl.BlockSpec(ovt.shape, lambda b, g, i: (0, 0)),
        ],
        out_specs=pl.BlockSpec((tq, gw), lambda b, g, i: (b * qt + i, g)),
        out_shape=jax.ShapeDtypeStruct((m, N_KV * gw), BF16),
        scratch_shapes=[
            pltpu.VMEM((seq_len, aug), BF16),
            pltpu.VMEM((seq_len + WINDOW, aug), BF16),
            pltpu.VMEM((rows, aug), BF16),
            pltpu.VMEM((rows, aug), BF16),
            pltpu.VMEM((1, rows), F32),
            pltpu.VMEM((V_ROWS, rows), F32),
            pltpu.VMEM((N_BRANCH, HEAD_DIM, rows), F32),
            pltpu.VMEM((tk, rows), F32),
            pltpu.VMEM((N_BRANCH, tq, gw), F32),
        ],
        compiler_params=_cparams(("parallel", "parallel", "arbitrary")),
        name="nsa_attention",
    )(proj, proj, proj, proj, proj, kcv, vct, kslab, vt, kslab, vt, ovt)


def _selection_overlap_t(n_cmp, ncp, n_slc, nsp):
    c0 = np.arange(ncp)[None, :] * CMP_STRIDE
    s0 = np.arange(nsp)[:, None] * SEL_LEN
    ov = np.clip(np.minimum(c0 + CMP_LEN, s0 + SEL_LEN) - np.maximum(c0, s0), 0, None) / CMP_STRIDE
    ov = ov * (np.arange(ncp)[None, :] < n_cmp) * (np.arange(nsp)[:, None] < n_slc)
    return jnp.asarray(ov, dtype=BF16)


def _s5_layer(x2d, mods, norm_g, w_in, layer, s5p, d_skip, w_glu, b_glu, w_out, batch, seq_len, final_g,
              final_norm):
    d = x2d.shape[1]
    shift, scale, gate = (mods[:, None, i * d:(i + 1) * d] for i in range(3))
    e = w_glu.shape[1]
    uz = _normmod_matmul(x2d, norm_g[None], shift, scale, w_in, layer, jnp.ones((1, 2 * e), F32),
                         seq_len, tm=PROJ_ROWS, tn=1024)
    wb, wc, a_re, a_im = s5p
    y = _s5_scan(uz, wb, wc, a_re, a_im, d_skip[None].astype(F32), batch, seq_len, steps=256)
    return _glu_out_residual(y, uz, w_glu, b_glu[None].astype(F32), w_out, layer, x2d, gate,
                             final_g[None], seq_len, tm=512, final_norm=final_norm)


def _qg_weight(w_qg):
    nl, d, _ = w_qg.shape
    att = N_HEADS * HEAD_DIM
    g_end = att + N_BRANCH * N_HEADS
    w_qg = w_qg.astype(BF16)
    wg = w_qg[:, :, att:g_end].reshape(nl, d, N_BRANCH, N_KV, HPG)
    wg = jnp.transpose(wg, (0, 1, 3, 2, 4)).reshape(nl, d, N_KV, N_BRANCH * HPG)
    wg = jnp.pad(wg, ((0, 0), (0, 0), (0, 0), (0, LANES - N_BRANCH * HPG))).reshape(nl, d, N_KV * LANES)
    w = jnp.concatenate([w_qg[:, :, :att], w_qg[:, :, g_end:], wg], axis=2)
    cs = jnp.concatenate([jnp.full((att,), HEAD_DIM ** -0.5, F32), jnp.ones((w.shape[2] - att,), F32)])
    return w, cs[None]


def kernel(x, c, norm_g, mod_w, mod_b, ssm_w_in, ssm_lam_re, ssm_lam_im, ssm_log_step, ssm_b_re, ssm_b_im, ssm_c_re, ssm_c_im, ssm_d, ssm_w_glu, ssm_b_glu, ssm_w_out, kv_norm_g, kv_mod_w, kv_mod_b, w_kv, cmp_pe, cmp_w1, cmp_b1, cmp_w2, cmp_b2, nsa_w_qg, nsa_w_o, final_norm_g):
    batch, seq_len, d = x.shape
    depth = mod_w.shape[0]
    n_a = ssm_w_in.shape[0]
    m = batch * seq_len
    x2d = x.reshape(m, d)

    c_pad = jnp.pad(c, ((0, 2 * SUBLANES - batch), (0, 0)))
    mods = _cond_matmul(c_pad, mod_w, mod_b[:, None])[:, :batch]
    kv_mods = _cond_matmul(c_pad, kv_mod_w[None], kv_mod_b[None, None])[0, :batch]

    w_in, w_glu, w_out = ssm_w_in.astype(BF16), ssm_w_glu.astype(BF16), ssm_w_out.astype(BF16)
    for layer in range(n_a):
        s5p = _s5_params(ssm_lam_re[layer], ssm_lam_im[layer], ssm_log_step[layer], ssm_b_re[layer],
                         ssm_b_im[layer], ssm_c_re[layer], ssm_c_im[layer])
        x2d = _s5_layer(x2d, mods[layer], norm_g[layer], w_in, layer, s5p, ssm_d[layer], w_glu,
                        ssm_b_glu[layer], w_out, batch, seq_len, final_norm_g,
                        final_norm=(layer == depth - 1))

    tq = KEY_TILE
    assert seq_len % KEY_TILE == 0 and WINDOW % KEY_TILE == 0 and WINDOW % MASK_BLOCK == 0
    kv_shift, kv_scale = kv_mods[:, None, :d], kv_mods[:, None, d:]
    kslab, vt = _kv_projection(x2d, kv_norm_g[None], kv_shift, kv_scale, w_kv.astype(BF16), batch, seq_len,
                               tm=PROJ_ROWS, tk=KEY_TILE)
    kslab = kslab.reshape(4 * N_KV, batch, seq_len, HEAD_DIM)

    n_cmp = (seq_len - CMP_LEN) // CMP_STRIDE + 1
    n_slc = seq_len // SEL_LEN
    nb16 = seq_len // CMP_STRIDE
    half = CMP_STRIDE * HEAD_DIM
    kv16 = kslab[:2 * N_KV].reshape(2 * N_KV, batch, nb16, half)
    w1cat = jnp.concatenate([cmp_w1[:, :half], cmp_w1[:, half:]], axis=2).astype(BF16)
    pe2 = jnp.pad(cmp_pe.reshape(2, 2, half), ((0, 0), (0, 2 * SUBLANES - 2), (0, 0)))
    kcv = _compress(kv16, w1cat, pe2, cmp_b1[:, None], cmp_w2.astype(BF16), cmp_b2[:, None], batch, n_cmp)

    nsp = FLAG_COL - HEAD_DIM
    assert n_slc <= nsp
    vct = jnp.swapaxes(kcv[1], -1, -2)
    ovt = _selection_overlap_t(n_cmp, nb16, n_slc, nsp)

    wq, cs = _qg_weight(nsa_w_qg)
    w_o = nsa_w_o.astype(BF16)
    for layer in range(n_a, depth):
        j = layer - n_a
        shift, scale, gate = (mods[layer][:, None, i * d:(i + 1) * d] for i in range(3))
        proj = _normmod_matmul(x2d, norm_g[layer][None], shift, scale, wq, j, cs, seq_len, tm=PROJ_ROWS, tn=512)
        o = _nsa_attention_t(proj, kcv, vct, kslab, vt, ovt, batch, seq_len, tq=tq)
        x2d = _mm_residual(o, w_o, j, x2d, gate, final_norm_g[None], seq_len, tm=512,
                           final_norm=(layer == depth - 1))

    return x2d.reshape(batch, seq_len, d)
```

```python
import functools
import math

import jax
import jax.numpy as jnp
import numpy as np
from jax import lax
from jax.experimental import pallas as pl
from jax.experimental.pallas import tpu as pltpu

F32 = jnp.float32
BF16 = jnp.bfloat16

SSM_GROUP = 16
SSM_STATE = 64
N_HEADS = 16
N_KV = 4
HPG = N_HEADS // N_KV
HEAD_DIM = 128
N_BRANCH = 3
CMP_LEN = 32
CMP_STRIDE = 16
SEL_LEN = 64
SEL_TOPK = 16
WINDOW = 512
SEL_BONUS = 1e3
NEG = -1e30
EPS = 1e-6

LANES = 128
SUBLANES = 8
MXU_DIM = 256
VMEM_LIMIT = 56 * 1024 * 1024

PROJ_ROWS = 1024

GROUPS_PER_BLOCK = MXU_DIM // SSM_GROUP
STATES_PER_BLOCK = GROUPS_PER_BLOCK * SSM_STATE


def _cparams(sem):
    return pltpu.CompilerParams(dimension_semantics=sem, vmem_limit_bytes=VMEM_LIMIT)


def _gelu_tanh(x):
    return x * (0.5 * (1.0 + jnp.tanh(math.sqrt(2.0 / math.pi) * (x + 0.044715 * (x * x * x)))))


def _sigmoid(x):
    return 0.5 * jnp.tanh(0.5 * x) + 0.5


def _dot_nt(a, b):
    return lax.dot_general(a, b, (((1,), (1,)), ((), ())), preferred_element_type=F32)


def _cond_kernel(c_ref, w_ref, b_ref, o_ref):
    c = c_ref[...]
    ca = (c * _sigmoid(c)).astype(BF16)
    acc = jnp.dot(ca, w_ref[0].astype(BF16), preferred_element_type=F32)
    o_ref[0] = acc + b_ref[0]


def _cond_matmul(c_pad, w, b, tn=1024):
    nl, d, n = w.shape
    r = c_pad.shape[0]
    return pl.pallas_call(
        _cond_kernel,
        grid=(nl, n // tn),
        in_specs=[
            pl.BlockSpec((r, d), lambda l, j: (0, 0)),
            pl.BlockSpec((1, d, tn), lambda l, j: (l, 0, j)),
            pl.BlockSpec((1, 1, tn), lambda l, j: (l, 0, j)),
        ],
        out_specs=pl.BlockSpec((1, r, tn), lambda l, j: (l, 0, j)),
        out_shape=jax.ShapeDtypeStruct((nl, r, n), F32),
        compiler_params=_cparams(("parallel", "parallel")),
        name="cond_matmul",
    )(c_pad, w, b)


NORM_SLAB = 16


def _norm_modulate(x_ref, g_ref, sh_ref, sc_ref, h_ref):
    gain = g_ref[...] * (1.0 + sc_ref[0])
    shift = sh_ref[0]

    def slab(r, carry):
        rs = pl.ds(pl.multiple_of(r * NORM_SLAB, NORM_SLAB), NORM_SLAB)
        x = x_ref[rs, :]
        y = x * lax.rsqrt(jnp.mean(x * x, axis=-1, keepdims=True) + EPS)
        h_ref[rs, :] = (y * gain + shift).astype(BF16)
        return carry

    lax.fori_loop(0, x_ref.shape[0] // NORM_SLAB, slab, 0, unroll=8)


def _normmod_mm_kernel(x_ref, g_ref, sh_ref, sc_ref, w_ref, cs_ref, o_ref, h_ref):
    @pl.when(pl.program_id(1) == 0)
    def _():
        _norm_modulate(x_ref, g_ref, sh_ref, sc_ref, h_ref)

    acc = jnp.dot(h_ref[...], w_ref[...], preferred_element_type=F32) * cs_ref[...]
    o_ref[...] = acc.astype(o_ref.dtype)


def _normmod_matmul(x2d, g, shift, scale, w, layer, colscale, seq_len, *, tm, tn):
    m, d = x2d.shape
    n = w.shape[2]
    rows_per_batch = seq_len // tm
    return pl.pallas_call(
        _normmod_mm_kernel,
        grid=(m // tm, n // tn),
        in_specs=[
            pl.BlockSpec((tm, d), lambda i, j: (i, 0)),
            pl.BlockSpec((1, d), lambda i, j: (0, 0)),
            pl.BlockSpec((1, 1, d), lambda i, j: (i // rows_per_batch, 0, 0)),
            pl.BlockSpec((1, 1, d), lambda i, j: (i // rows_per_batch, 0, 0)),
            pl.BlockSpec((None, d, tn), lambda i, j: (layer, 0, j)),
            pl.BlockSpec((1, tn), lambda i, j: (0, j)),
        ],
        out_specs=pl.BlockSpec((tm, tn), lambda i, j: (i, j)),
        out_shape=jax.ShapeDtypeStruct((m, n), BF16),
        scratch_shapes=[pltpu.VMEM((tm, d), BF16)],
        compiler_params=_cparams(("parallel", "arbitrary")),
        name="normmod_matmul",
    )(x2d, g, shift, scale, w, colscale)


KV_KCMP, KV_VCMP, KV_KSEL, KV_KWIN = range(4)
VT_SEL, VT_WIN = range(2)
_SEL_V_COL, _WIN_V_COL = 3, 5


def _kv_proj_kernel(x_ref, g_ref, sh_ref, sc_ref, w_ref, ks_ref, vt_ref, h_ref):
    j = pl.program_id(1)

    @pl.when(j == 0)
    def _():
        _norm_modulate(x_ref, g_ref, sh_ref, sc_ref, h_ref)

    acc = jnp.dot(h_ref[...], w_ref[...], preferred_element_type=F32)
    tk = vt_ref.shape[-1]
    is_value = (j == _SEL_V_COL) | (j == _WIN_V_COL)

    @pl.when(jnp.logical_not(is_value))
    def _():
        for s in range(N_KV):
            ks_ref[s] = acc[:, s * HEAD_DIM:(s + 1) * HEAD_DIM].astype(ks_ref.dtype)

    @pl.when(is_value)
    def _():
        for s in range(N_KV):
            for r in range(acc.shape[0] // tk):
                blk = acc[r * tk:(r + 1) * tk, s * HEAD_DIM:(s + 1) * HEAD_DIM]
                vt_ref[s, 0, r] = blk.T.astype(vt_ref.dtype)


def _kv_projection(x2d, g, shift, scale, w, batch, seq_len, *, tm, tk):
    m, d = x2d.shape
    tn = N_KV * HEAD_DIM
    assert w.shape[1] == 6 * tn
    rpb = seq_len // tm

    def slab_kind(j):
        return j - (j >= _SEL_V_COL).astype(jnp.int32) - (j >= _WIN_V_COL).astype(jnp.int32)

    return pl.pallas_call(
        _kv_proj_kernel,
        grid=(m // tm, 6),
        in_specs=[
            pl.BlockSpec((tm, d), lambda i, j: (i, 0)),
            pl.BlockSpec((1, d), lambda i, j: (0, 0)),
            pl.BlockSpec((1, 1, d), lambda i, j: (i // rpb, 0, 0)),
            pl.BlockSpec((1, 1, d), lambda i, j: (i // rpb, 0, 0)),
            pl.BlockSpec((d, tn), lambda i, j: (0, j)),
        ],
        out_specs=[
            pl.BlockSpec((N_KV, tm, HEAD_DIM), lambda i, j: (slab_kind(j), i, 0)),
            pl.BlockSpec((N_KV, 1, tm // tk, HEAD_DIM, tk),
                         lambda i, j: ((j > _SEL_V_COL).astype(jnp.int32), i // rpb, i % rpb, 0, 0)),
        ],
        out_shape=[
            jax.ShapeDtypeStruct((4 * N_KV, m, HEAD_DIM), BF16),
            jax.ShapeDtypeStruct((2 * N_KV, batch, seq_len // tk, HEAD_DIM, tk), BF16),
        ],
        scratch_shapes=[pltpu.VMEM((tm, d), BF16)],
        compiler_params=_cparams(("parallel", "arbitrary")),
        name="kv_projection",
    )(x2d, g, shift, scale, w)


def _mm_res_kernel(a_ref, w_ref, x_ref, gate_ref, fg_ref, o_ref, *, final_norm):
    acc = jnp.dot(a_ref[...], w_ref[...], preferred_element_type=F32)
    xn = x_ref[...] + gate_ref[0] * acc
    if final_norm:
        xn = xn * lax.rsqrt(jnp.mean(xn * xn, axis=-1, keepdims=True) + EPS) * fg_ref[...]
    o_ref[...] = xn


def _mm_residual(a, w, layer, x2d, gate, final_g, seq_len, *, tm, final_norm):
    m, k = a.shape
    n = w.shape[2]
    rows_per_batch = seq_len // tm
    return pl.pallas_call(
        functools.partial(_mm_res_kernel, final_norm=final_norm),
        grid=(m // tm,),
        in_specs=[
            pl.BlockSpec((tm, k), lambda i: (i, 0)),
            pl.BlockSpec((None, k, n), lambda i: (layer, 0, 0), pipeline_mode=pl.Buffered(1)),
            pl.BlockSpec((tm, n), lambda i: (i, 0)),
            pl.BlockSpec((1, 1, n), lambda i: (i // rows_per_batch, 0, 0)),
            pl.BlockSpec((1, n), lambda i: (0, 0)),
        ],
        out_specs=pl.BlockSpec((tm, n), lambda i: (i, 0)),
        out_shape=jax.ShapeDtypeStruct((m, n), F32),
        compiler_params=_cparams(("parallel",)),
        name="matmul_residual_final" if final_norm else "matmul_residual",
    )(a, w, x2d, gate, final_g)


def _glu_out_kernel(y_ref, z_ref, wg_ref, bg_ref, wo_ref, x_ref, gate_ref, fg_ref, o_ref, *, final_norm):
    y = y_ref[...]
    lin = jnp.dot(y, wg_ref[...], preferred_element_type=F32) + bg_ref[...]
    z = z_ref[...].astype(F32)
    t = ((y.astype(F32) * _sigmoid(lin)) * (z * _sigmoid(z))).astype(BF16)
    xn = x_ref[...] + gate_ref[0] * jnp.dot(t, wo_ref[...], preferred_element_type=F32)
    if final_norm:
        xn = xn * lax.rsqrt(jnp.mean(xn * xn, axis=-1, keepdims=True) + EPS) * fg_ref[...]
    o_ref[...] = xn


def _glu_out_residual(y, uz, w_glu, b_glu, w_out, layer, x2d, gate, final_g, seq_len, *, tm, final_norm):
    m, e = y.shape
    n = w_out.shape[2]
    rows_per_batch = seq_len // tm
    resident = pl.Buffered(1)
    return pl.pallas_call(
        functools.partial(_glu_out_kernel, final_norm=final_norm),
        grid=(m // tm,),
        in_specs=[
            pl.BlockSpec((tm, e), lambda i: (i, 0)),
            pl.BlockSpec((tm, e), lambda i: (i, 1)),
            pl.BlockSpec((None, e, e), lambda i: (layer, 0, 0), pipeline_mode=resident),
            pl.BlockSpec((1, e), lambda i: (0, 0)),
            pl.BlockSpec((None, e, n), lambda i: (layer, 0, 0), pipeline_mode=resident),
            pl.BlockSpec((tm, n), lambda i: (i, 0)),
            pl.BlockSpec((1, 1, n), lambda i: (i // rows_per_batch, 0, 0)),
            pl.BlockSpec((1, n), lambda i: (0, 0)),
        ],
        out_specs=pl.BlockSpec((tm, n), lambda i: (i, 0)),
        out_shape=jax.ShapeDtypeStruct((m, n), F32),
        compiler_params=_cparams(("parallel",)),
        name="s5_glu_out",
    )(y, uz, w_glu, b_glu, w_out, x2d, gate, final_g)


SCAN_ROW_PITCH = 12


def _s5_scan_kernel(u_ref, wb_ref, wc_ref, are_ref, aim_ref, d_ref, y_ref, s_ref, st_ref, *, steps, nblk):
    nslab = STATES_PER_BLOCK // LANES

    @pl.when(pl.program_id(1) == 0)
    def _():
        st_ref[...] = jnp.zeros_like(st_ref)

    for blk in range(nblk):
        bu = jnp.dot(u_ref[:, blk * MXU_DIM:(blk + 1) * MXU_DIM], wb_ref[blk], preferred_element_type=F32)
        for k in range(2 * nslab):
            s_ref[k, pl.ds(blk, steps, stride=SCAN_ROW_PITCH), :] = bu[:, k * LANES:(k + 1) * LANES]

    a_re = [are_ref[:, k * LANES:(k + 1) * LANES] for k in range(nslab)]
    a_im = [aim_ref[:, k * LANES:(k + 1) * LANES] for k in range(nslab)]

    def step(t, carry):
        xr, xi = carry
        r0 = t * SCAN_ROW_PITCH
        nr, ni = [], []
        for k in range(nslab):
            b_re = s_ref[k, pl.ds(r0, nblk), :]
            b_im = s_ref[nslab + k, pl.ds(r0, nblk), :]
            v_re = a_re[k] * xr[k] - a_im[k] * xi[k] + b_re
            v_im = a_re[k] * xi[k] + a_im[k] * xr[k] + b_im
            s_ref[k, pl.ds(r0, nblk), :] = v_re
            s_ref[nslab + k, pl.ds(r0, nblk), :] = v_im
            nr.append(v_re)
            ni.append(v_im)
        return tuple(nr), tuple(ni)

    init = (tuple(st_ref[k] for k in range(nslab)), tuple(st_ref[nslab + k] for k in range(nslab)))
    xr, xi = lax.fori_loop(0, steps, step, init, unroll=8)
    for k in range(nslab):
        st_ref[k] = xr[k]
        st_ref[nslab + k] = xi[k]

    for blk in range(nblk):
        xs = jnp.concatenate(
            [s_ref[k, pl.ds(blk, steps, stride=SCAN_ROW_PITCH), :].astype(BF16) for k in range(2 * nslab)], axis=1)
        cols = slice(blk * MXU_DIM, (blk + 1) * MXU_DIM)
        yb = jnp.dot(xs, wc_ref[blk], preferred_element_type=F32)
        yb = yb + d_ref[:, cols] * u_ref[:, cols].astype(F32)
        y_ref[:, cols] = _gelu_tanh(yb).astype(y_ref.dtype)


def _s5_scan(uz, wb, wc, a_re, a_im, d_skip, batch, seq_len, *, steps):
    m = uz.shape[0]
    nblk, _, two_states = wb.shape
    e = nblk * MXU_DIM
    chunks = seq_len // steps
    nslab2 = two_states // LANES
    return pl.pallas_call(
        functools.partial(_s5_scan_kernel, steps=steps, nblk=nblk),
        grid=(batch, chunks),
        in_specs=[
            pl.BlockSpec((steps, e), lambda b, c: (b * chunks + c, 0)),
            pl.BlockSpec(wb.shape, lambda b, c: (0, 0, 0), pipeline_mode=pl.Buffered(1)),
            pl.BlockSpec(wc.shape, lambda b, c: (0, 0, 0), pipeline_mode=pl.Buffered(1)),
            pl.BlockSpec(a_re.shape, lambda b, c: (0, 0)),
            pl.BlockSpec(a_im.shape, lambda b, c: (0, 0)),
            pl.BlockSpec((1, e), lambda b, c: (0, 0)),
        ],
        out_specs=pl.BlockSpec((steps, e), lambda b, c: (b * chunks + c, 0)),
        out_shape=jax.ShapeDtypeStruct((m, e), BF16),
        scratch_shapes=[
            pltpu.VMEM((nslab2, steps * SCAN_ROW_PITCH, LANES), F32),
            pltpu.VMEM((nslab2, nblk, LANES), F32),
        ],
        compiler_params=_cparams(("parallel", "arbitrary")),
        name="s5_scan",
    )(uz, wb, wc, a_re, a_im, d_skip)


def _s5_params(lam_re, lam_im, log_step, b_re, b_im, c_re, c_im):
    g, n = lam_re.shape
    dt = jnp.exp(log_step.astype(F32))[:, None]
    lr, li = lam_re.astype(F32), lam_im.astype(F32)
    mag = jnp.exp(lr * dt)
    a_re, a_im = mag * jnp.cos(li * dt), mag * jnp.sin(li * dt)
    den = lr * lr + li * li
    coef_re = ((a_re - 1.0) * lr + a_im * li) / den
    coef_im = (a_im * lr - (a_re - 1.0) * li) / den
    br, bi = b_re.astype(F32), b_im.astype(F32)
    bb_re = coef_re[..., None] * br - coef_im[..., None] * bi
    bb_im = coef_re[..., None] * bi + coef_im[..., None] * br
    nblk = g // GROUPS_PER_BLOCK

    def blockdiag(t, inner):
        rows = t.shape[1]
        cols = GROUPS_PER_BLOCK * inner
        expand = (jnp.arange(cols)[None, :] % inner == jnp.arange(inner)[:, None]).astype(F32)
        tiled = jnp.einsum('brn,nk->brk', t, expand, precision=lax.Precision.HIGHEST)
        same_group = (jnp.arange(rows)[:, None] // (rows // GROUPS_PER_BLOCK)
                      == jnp.arange(cols)[None, :] // inner)
        return jnp.where(same_group, tiled, 0.0)

    def blockdiag_in(bb):
        t = jnp.swapaxes(bb.reshape(nblk, GROUPS_PER_BLOCK, n, SSM_GROUP), 2, 3)
        return blockdiag(t.reshape(nblk, GROUPS_PER_BLOCK * SSM_GROUP, n), n)

    def blockdiag_out(cc):
        t = jnp.swapaxes(cc.astype(F32).reshape(nblk, GROUPS_PER_BLOCK, SSM_GROUP, n), 2, 3)
        return blockdiag(t.reshape(nblk, GROUPS_PER_BLOCK * n, SSM_GROUP), SSM_GROUP)

    wb = jnp.concatenate([blockdiag_in(bb_re), blockdiag_in(bb_im)], axis=2).astype(BF16)
    wc = jnp.concatenate([blockdiag_out(c_re), -blockdiag_out(c_im)], axis=1).astype(BF16)
    return wb, wc, a_re.reshape(nblk, GROUPS_PER_BLOCK * n), a_im.reshape(nblk, GROUPS_PER_BLOCK * n)


def _compress_kernel(x_ref, w1_ref, pe_ref, b1_ref, w2_ref, b2_ref, o_ref, sh_ref, *, n_cmp):
    nb = x_ref.shape[2]
    w1 = w1_ref[0]
    r = jnp.dot(x_ref[0, 0], w1, preferred_element_type=F32)
    pe = pe_ref[0]
    pe_hi = pe.astype(BF16)
    pe_lo = (pe - pe_hi.astype(F32)).astype(BF16)
    pw = (jnp.dot(pe_hi, w1, preferred_element_type=F32) + jnp.dot(pe_lo, w1, preferred_element_type=F32))
    const = pw[0:1, :HEAD_DIM] + pw[1:2, HEAD_DIM:] + b1_ref[0]
    sh_ref[pl.ds(0, nb), :] = r[:, HEAD_DIM:]
    sh_ref[pl.ds(nb, SUBLANES), :] = jnp.zeros((SUBLANES, HEAD_DIM), F32)
    hid = r[:, :HEAD_DIM] + sh_ref[pl.ds(1, nb), :] + const
    out = jnp.dot(_gelu_tanh(hid).astype(BF16), w2_ref[0], preferred_element_type=F32) + b2_ref[0]
    row = lax.broadcasted_iota(jnp.int32, out.shape, 0)
    o_ref[0, 0, 0] = jnp.where(row < n_cmp, out, 0.0).astype(o_ref.dtype)


def _compress(kv16, w1cat, pe2, b1, w2, b2, batch, n_cmp):
    nb, wide = kv16.shape[2], kv16.shape[3]
    return pl.pallas_call(
        functools.partial(_compress_kernel, n_cmp=n_cmp),
        grid=(2, N_KV, batch),
        in_specs=[
            pl.BlockSpec((1, 1, nb, wide), lambda s, g, b: (s * N_KV + g, b, 0, 0)),
            pl.BlockSpec((1, wide, 2 * HEAD_DIM), lambda s, g, b: (s, 0, 0)),
            pl.BlockSpec((1, 2 * SUBLANES, wide), lambda s, g, b: (s, 0, 0)),
            pl.BlockSpec((1, 1, HEAD_DIM), lambda s, g, b: (s, 0, 0)),
            pl.BlockSpec((1, HEAD_DIM, HEAD_DIM), lambda s, g, b: (s, 0, 0)),
            pl.BlockSpec((1, 1, HEAD_DIM), lambda s, g, b: (s, 0, 0)),
        ],
        out_specs=pl.BlockSpec((1, 1, 1, nb, HEAD_DIM), lambda s, g, b: (s, b, g, 0, 0)),
        out_shape=jax.ShapeDtypeStruct((2, batch, N_KV, nb, HEAD_DIM), BF16),
        scratch_shapes=[pltpu.VMEM((nb + SUBLANES, HEAD_DIM), F32)],
        compiler_params=_cparams(("parallel", "parallel", "parallel")),
        name="nsa_compress",
    )(kv16, w1cat, pe2, b1, w2, b2)


FLAG_COL = HEAD_DIM + HEAD_DIM // 2
V_ROWS = HEAD_DIM + 16
KEY_TILE = 512
MASK_BLOCK = 128


def _blocked_probs(s, tq, kind_of, m_prev=None):
    nkb, nlt, nj = s.shape[0] // MASK_BLOCK, s.shape[1] // MASK_BLOCK, tq // MASK_BLOCK
    krow = lax.broadcasted_iota(jnp.int32, (MASK_BLOCK, MASK_BLOCK), 0)
    tcol = lax.broadcasted_iota(jnp.int32, (MASK_BLOCK, MASK_BLOCK), 1)
    visible = {"le": krow <= tcol, "gt": krow > tcol}
    zero = jnp.zeros((MASK_BLOCK, MASK_BLOCK), BF16)
    ms, cols = [], []
    for c in range(nlt):
        ls = slice(c * MASK_BLOCK, (c + 1) * MASK_BLOCK)
        blks = {}
        for kk in range(nkb):
            kind = kind_of(kk, c % nj)
            if kind is not None:
                b = s[kk * MASK_BLOCK:(kk + 1) * MASK_BLOCK, ls]
                blks[kk] = b if kind == "full" else jnp.where(visible[kind], b, NEG)
        m_c = jnp.max(functools.reduce(jnp.maximum, blks.values()), axis=0, keepdims=True)
        if m_prev is not None:
            m_c = jnp.maximum(m_c, m_prev[:, ls])
        ms.append(m_c)
        cols.append(jnp.concatenate(
            [jnp.exp((blks[kk] - m_c).astype(BF16)) if kk in blks else zero for kk in range(nkb)], axis=0))
    return jnp.concatenate(ms, axis=1), jnp.concatenate(cols, axis=1)


def _nsa_t_kernel(q_ref, z0_ref, z1_ref, z2_ref, gt_ref, kc_ref, vct_ref, ks_ref, vst_ref, kw_ref, vwt_ref,
                  ovt_ref, o_ref, ka_ref, kwa_ref, qa_ref, qw_ref, m_ref, acc_ref, ot_ref, s_ref, gz_ref,
                  *, tq, n_cmp, n_slc, topk):
    qi = pl.program_id(2)
    t0 = qi * tq
    rows = HPG * tq
    nsp = ovt_ref.shape[0]
    seq_len, aug = ka_ref.shape

    @pl.when(qi == 0)
    def _():
        key_blk = lax.broadcasted_iota(jnp.int32, (seq_len, aug - HEAD_DIM), 0) // SEL_LEN
        col = lax.broadcasted_iota(jnp.int32, (seq_len, aug - HEAD_DIM), 1)
        ka_ref[:, 0:HEAD_DIM] = ks_ref[0, 0]
        ka_ref[:, HEAD_DIM:] = jnp.where(key_blk == col, 1.0, 0.0).astype(BF16)
        pcol = lax.broadcasted_iota(jnp.int32, (WINDOW, aug), 1)
        kwa_ref[0:WINDOW, :] = jnp.where(pcol == FLAG_COL, 1.0, 0.0).astype(BF16)
        kwa_ref[WINDOW:, 0:HEAD_DIM] = kw_ref[0, 0]
        kwa_ref[WINDOW:, HEAD_DIM:] = jnp.zeros((seq_len, aug - HEAD_DIM), BF16)

    ones_rows = jnp.where(lax.broadcasted_iota(jnp.int32, (V_ROWS - HEAD_DIM, vst_ref.shape[4]), 0) == 0,
                          1.0, 0.0).astype(BF16)

    def v_tile(vt_ref, idx):
        return jnp.concatenate([vt_ref[0, 0, idx], ones_rows], axis=0)

    q = q_ref[...]
    qh = jnp.concatenate([q[:, h * HEAD_DIM:(h + 1) * HEAD_DIM] for h in range(HPG)], axis=0)
    flag = jnp.where(lax.broadcasted_iota(jnp.int32, (tq, aug - FLAG_COL), 1) == 0, NEG, 0.0).astype(BF16)
    qa_ref[:, 0:HEAD_DIM] = qh
    qw_ref[:, 0:HEAD_DIM] = qh
    qw_ref[:, HEAD_DIM:FLAG_COL] = jnp.zeros((rows, FLAG_COL - HEAD_DIM), BF16)
    for h in range(HPG):
        qa_ref[h * tq:(h + 1) * tq, FLAG_COL:] = flag
        qw_ref[h * tq:(h + 1) * tq, FLAG_COL:] = flag

    tk = vst_ref.shape[4]
    n_win = WINDOW // MASK_BLOCK

    def window_kind(kk, j):
        d = kk - j
        return "gt" if d == 0 else "le" if d == n_win else "full" if 0 < d < n_win else None

    sw = _dot_nt(kwa_ref[pl.ds(pl.multiple_of(t0, tq), WINDOW + tq), :], qw_ref[...])
    _, pw = _blocked_probs(sw, tq, window_kind)
    o_w = jnp.zeros((V_ROWS, rows), F32)
    for dd in range((WINDOW + tq) // tk):
        vidx = jnp.maximum(qi * (tq // tk) + dd - WINDOW // tk, 0)
        o_w = o_w + jnp.dot(v_tile(vwt_ref, vidx), pw[dd * tk:(dd + 1) * tk], preferred_element_type=F32)
    ot_ref[2] = o_w[:HEAD_DIM] * (1.0 / o_w[HEAD_DIM:HEAD_DIM + 1])

    gates = _sigmoid(gt_ref[...].astype(F32))
    for br, z_ref in enumerate((z0_ref, z1_ref, z2_ref)):
        for h in range(HPG):
            cs = slice(h * HEAD_DIM, (h + 1) * HEAD_DIM)
            z = z_ref[:, cs].astype(F32)
            c = br * HPG + h
            gz_ref[br, :, cs] = gates[:, c:c + 1] * (z * _sigmoid(z))

    ncp = kc_ref.shape[3]
    st = _dot_nt(kc_ref[0, 0, 0], qh)
    n_idx = lax.broadcasted_iota(jnp.int32, (ncp, rows), 0)
    tpos = t0 + lax.broadcasted_iota(jnp.int32, (ncp, rows), 1) % tq
    cmask = (n_idx * CMP_STRIDE + (CMP_LEN - 1) <= tpos) & (n_idx < n_cmp)
    st = jnp.where(cmask, st, NEG)
    e = jnp.where(cmask, jnp.exp(st - jnp.max(st, axis=0, keepdims=True)), 0.0)
    l = jnp.sum(e, axis=0, keepdims=True)
    p = e * (1.0 / jnp.where(l > 0.0, l, 1.0))
    ot_ref[0] = jnp.dot(vct_ref[0, 0], p.astype(BF16), preferred_element_type=F32)
    p_cmp = p[:, 0:tq]
    for h in range(1, HPG):
        p_cmp = p_cmp + p[:, h * tq:(h + 1) * tq]

    p_hi = p_cmp.astype(BF16)
    r1 = p_cmp - p_hi.astype(F32)
    p_mid = r1.astype(BF16)
    p_lo = (r1 - p_mid.astype(F32)).astype(BF16)
    ovt = ovt_ref[...]
    p_slc = (jnp.dot(ovt, p_hi, preferred_element_type=F32) + jnp.dot(ovt, p_mid, preferred_element_type=F32)
             + jnp.dot(ovt, p_lo, preferred_element_type=F32))
    blk = lax.broadcasted_iota(jnp.int32, (nsp, tq), 0)
    cur = (t0 + lax.broadcasted_iota(jnp.int32, (nsp, tq), 1)) // SEL_LEN
    valid = (blk <= cur) & (blk < n_slc)
    forced = (blk == 0) | (blk == cur) | (blk == cur - 1)
    score = jnp.where(valid, p_slc + jnp.where(forced, SEL_BONUS, 0.0), -SEL_BONUS)
    nv = nsp // SUBLANES
    sc = [score[k * SUBLANES:(k + 1) * SUBLANES] for k in range(nv)]
    sub = lax.broadcasted_iota(jnp.int32, (SUBLANES, tq), 0)
    rank = [jnp.zeros((SUBLANES, tq), jnp.int32) for _ in range(nv)]
    for i in range(n_slc):
        si = jnp.broadcast_to(score[i:i + 1, :], (SUBLANES, tq))
        for k in range(nv):
            if k * SUBLANES > i:
                beats = si >= sc[k]
            elif k * SUBLANES + SUBLANES - 1 <= i:
                beats = si > sc[k]
            else:
                beats = (si > sc[k]) | ((si == sc[k]) & (sub > i - k * SUBLANES))
            rank[k] = rank[k] + jnp.where(beats, 1, 0)
    rank = jnp.concatenate(rank, axis=0)
    sel = (rank < topk) & (score > -0.5 * SEL_BONUS)
    selb = jnp.where(sel, 0.0, NEG).T.astype(BF16)
    for h in range(HPG):
        qa_ref[h * tq:(h + 1) * tq, HEAD_DIM:FLAG_COL] = selb


    def scores(k_tile):
        return _dot_nt(k_tile, qa_ref[...])

    def online_update(s, vt_tile):
        m_old = m_ref[...]
        m_new = jnp.maximum(m_old, jnp.max(s, axis=0, keepdims=True))
        alpha = jnp.exp(m_old - m_new)
        pt = jnp.exp((s - m_new).astype(BF16))
        acc_ref[...] = alpha * acc_ref[...] + jnp.dot(vt_tile, pt, preferred_element_type=F32)
        m_ref[...] = m_new

    m_ref[...] = jnp.full(m_ref.shape, NEG, F32)
    acc_ref[...] = jnp.zeros(acc_ref.shape, F32)
    n_full = t0 // tk
    s_ref[...] = scores(ka_ref[pl.ds(0, tk), :])

    def sel_body(kt, carry):
        k1 = pl.multiple_of((kt + 1) * tk, tk)
        s_next = scores(ka_ref[pl.ds(k1, tk), :])
        online_update(s_ref[...], v_tile(vst_ref, kt))
        s_ref[...] = s_next
        return carry

    lax.fori_loop(0, n_full, sel_body, 0)

    def diagonal_kind(kk, j):
        return "full" if kk < j else "le" if kk == j else None

    m_old = m_ref[...]
    m_new, pt = _blocked_probs(s_ref[...], tq, diagonal_kind, m_old)
    acc = jnp.exp(m_old - m_new) * acc_ref[...] + jnp.dot(v_tile(vst_ref, n_full), pt, preferred_element_type=F32)
    ot_ref[1] = acc[:HEAD_DIM] * (1.0 / acc[HEAD_DIM:HEAD_DIM + 1])

    for h in range(HPG):
        cs = slice(h * HEAD_DIM, (h + 1) * HEAD_DIM)
        tot = None
        for br in range(N_BRANCH):
            term = gz_ref[br, :, cs] * ot_ref[br, :, h * tq:(h + 1) * tq].T
            tot = term if tot is None else tot + term
        o_ref[:, cs] = tot.astype(o_ref.dtype)


def _nsa_attention_t(proj, kcv, vct, kslab, vt, ovt, batch, seq_len, *, tq):
    m = proj.shape[0]
    gw = HPG * HEAD_DIM
    qt = seq_len // tq
    n_cmp = (seq_len - CMP_LEN) // CMP_STRIDE + 1
    n_slc = seq_len // SEL_LEN
    z_base = N_KV
    gate_base = (N_KV + N_BRANCH * N_KV) * (gw // LANES)
    ncp = kcv.shape[3]
    aug = 2 * HEAD_DIM
    rows = HPG * tq
    tk = vt.shape[-1]

    def zspec(br):
        return pl.BlockSpec((tq, gw), lambda b, g, i: (b * qt + i, z_base + br * N_KV + g))

    def per_group(arr, kind):
        return pl.BlockSpec((1, 1) + arr.shape[2:], lambda b, g, i: (kind * N_KV + g, b) + (0,) * (arr.ndim - 2))

    return pl.pallas_call(
        functools.partial(_nsa_t_kernel, tq=tq, n_cmp=n_cmp, n_slc=n_slc, topk=min(SEL_TOPK, n_slc)),
        grid=(batch, N_KV, qt),
        in_specs=[
            pl.BlockSpec((tq, gw), lambda b, g, i: (b * qt + i, g)),
            zspec(0), zspec(1), zspec(2),
            pl.BlockSpec((tq, LANES), lambda b, g, i: (b * qt + i, gate_base + g)),
            pl.BlockSpec((1, 1, 1, ncp, HEAD_DIM), lambda b, g, i: (0, b, g, 0, 0)),
            pl.BlockSpec((1, 1, HEAD_DIM, ncp), lambda b, g, i: (b, g, 0, 0)),
            per_group(kslab, KV_KSEL), per_group(vt, VT_SEL), per_group(kslab, KV_KWIN), per_group(vt, VT_WIN),
            pl.BlockSpec(ovt.shape, lambda b, g, i: (0, 0)),
        ],
        out_specs=pl.BlockSpec((tq, gw), lambda b, g, i: (b * qt + i, g)),
        out_shape=jax.ShapeDtypeStruct((m, N_KV * gw), BF16),
        scratch_shapes=[
            pltpu.VMEM((seq_len, aug), BF16),
            pltpu.VMEM((seq_len + WINDOW, aug), BF16),
            pltpu.VMEM((rows, aug), BF16),
            pltpu.VMEM((rows, aug), BF16),
            pltpu.VMEM((1, rows), F32),
            pltpu.VMEM((V_ROWS, rows), F32),
            pltpu.VMEM((N_BRANCH, HEAD_DIM, rows), F32),
            pltpu.VMEM((tk, rows), F32),
            pltpu.VMEM((N_BRANCH, tq, gw), F32),
        ],
        compiler_params=_cparams(("parallel", "parallel", "arbitrary")),
        name="nsa_attention",
    )(proj, proj, proj, proj, proj, kcv, vct, kslab, vt, kslab, vt, ovt)


def _selection_overlap_t(n_cmp, ncp, n_slc, nsp):
    c0 = np.arange(ncp)[None, :] * CMP_STRIDE
    s0 = np.arange(nsp)[:, None] * SEL_LEN
    ov = np.clip(np.minimum(c0 + CMP_LEN, s0 + SEL_LEN) - np.maximum(c0, s0), 0, None) / CMP_STRIDE
    ov = ov * (np.arange(ncp)[None, :] < n_cmp) * (np.arange(nsp)[:, None] < n_slc)
    return jnp.asarray(ov, dtype=BF16)


def _s5_layer(x2d, mods, norm_g, w_in, layer, s5p, d_skip, w_glu, b_glu, w_out, batch, seq_len, final_g,
              final_norm):
    d = x2d.shape[1]
    shift, scale, gate = (mods[:, None, i * d:(i + 1) * d] for i in range(3))
    e = w_glu.shape[1]
    uz = _normmod_matmul(x2d, norm_g[None], shift, scale, w_in, layer, jnp.ones((1, 2 * e), F32),
                         seq_len, tm=PROJ_ROWS, tn=1024)
    wb, wc, a_re, a_im = s5p
    y = _s5_scan(uz, wb, wc, a_re, a_im, d_skip[None].astype(F32), batch, seq_len, steps=256)
    return _glu_out_residual(y, uz, w_glu, b_glu[None].astype(F32), w_out, layer, x2d, gate,
                             final_g[None], seq_len, tm=512, final_norm=final_norm)


def _qg_weight(w_qg):
    nl, d, _ = w_qg.shape
    att = N_HEADS * HEAD_DIM
    g_end = att + N_BRANCH * N_HEADS
    w_qg = w_qg.astype(BF16)
    wg = w_qg[:, :, att:g_end].reshape(nl, d, N_BRANCH, N_KV, HPG)
    wg = jnp.transpose(wg, (0, 1, 3, 2, 4)).reshape(nl, d, N_KV, N_BRANCH * HPG)
    wg = jnp.pad(wg, ((0, 0), (0, 0), (0, 0), (0, LANES - N_BRANCH * HPG))).reshape(nl, d, N_KV * LANES)
    w = jnp.concatenate([w_qg[:, :, :att], w_qg[:, :, g_end:], wg], axis=2)
    cs = jnp.concatenate([jnp.full((att,), HEAD_DIM ** -0.5, F32), jnp.ones((w.shape[2] - att,), F32)])
    return w, cs[None]


def kernel(x, c, norm_g, mod_w, mod_b, ssm_w_in, ssm_lam_re, ssm_lam_im, ssm_log_step, ssm_b_re, ssm_b_im, ssm_c_re, ssm_c_im, ssm_d, ssm_w_glu, ssm_b_glu, ssm_w_out, kv_norm_g, kv_mod_w, kv_mod_b, w_kv, cmp_pe, cmp_w1, cmp_b1, cmp_w2, cmp_b2, nsa_w_qg, nsa_w_o, final_norm_g):
    batch, seq_len, d = x.shape
    depth = mod_w.shape[0]
    n_a = ssm_w_in.shape[0]
    m = batch * seq_len
    x2d = x.reshape(m, d)

    c_pad = jnp.pad(c, ((0, 2 * SUBLANES - batch), (0, 0)))
    mods = _cond_matmul(c_pad, mod_w, mod_b[:, None])[:, :batch]
    kv_mods = _cond_matmul(c_pad, kv_mod_w[None], kv_mod_b[None, None])[0, :batch]

    w_in, w_glu, w_out = ssm_w_in.astype(BF16), ssm_w_glu.astype(BF16), ssm_w_out.astype(BF16)
    for layer in range(n_a):
        s5p = _s5_params(ssm_lam_re[layer], ssm_lam_im[layer], ssm_log_step[layer], ssm_b_re[layer],
                         ssm_b_im[layer], ssm_c_re[layer], ssm_c_im[layer])
        x2d = _s5_layer(x2d, mods[layer], norm_g[layer], w_in, layer, s5p, ssm_d[layer], w_glu,
                        ssm_b_glu[layer], w_out, batch, seq_len, final_norm_g,
                        final_norm=(layer == depth - 1))

    tq = KEY_TILE
    assert seq_len % KEY_TILE == 0 and WINDOW % KEY_TILE == 0 and WINDOW % MASK_BLOCK == 0
    kv_shift, kv_scale = kv_mods[:, None, :d], kv_mods[:, None, d:]
    kslab, vt = _kv_projection(x2d, kv_norm_g[None], kv_shift, kv_scale, w_kv.astype(BF16), batch, seq_len,
                               tm=PROJ_ROWS, tk=KEY_TILE)
    kslab = kslab.reshape(4 * N_KV, batch, seq_len, HEAD_DIM)

    n_cmp = (seq_len - CMP_LEN) // CMP_STRIDE + 1
    n_slc = seq_len // SEL_LEN
    nb16 = seq_len // CMP_STRIDE
    half = CMP_STRIDE * HEAD_DIM
    kv16 = kslab[:2 * N_KV].reshape(2 * N_KV, batch, nb16, half)
    w1cat = jnp.concatenate([cmp_w1[:, :half], cmp_w1[:, half:]], axis=2).astype(BF16)
    pe2 = jnp.pad(cmp_pe.reshape(2, 2, half), ((0, 0), (0, 2 * SUBLANES - 2), (0, 0)))
    kcv = _compress(kv16, w1cat, pe2, cmp_b1[:, None], cmp_w2.astype(BF16), cmp_b2[:, None], batch, n_cmp)

    nsp = FLAG_COL - HEAD_DIM
    assert n_slc <= nsp
    vct = jnp.swapaxes(kcv[1], -1, -2)
    ovt = _selection_overlap_t(n_cmp, nb16, n_slc, nsp)

    wq, cs = _qg_weight(nsa_w_qg)
    w_o = nsa_w_o.astype(BF16)
    for layer in range(n_a, depth):
        j = layer - n_a
        shift, scale, gate = (mods[layer][:, None, i * d:(i + 1) * d] for i in range(3))
        proj = _normmod_matmul(x2d, norm_g[layer][None], shift, scale, wq, j, cs, seq_len, tm=PROJ_ROWS, tn=512)
        o = _nsa_attention_t(proj, kcv, vct, kslab, vt, ovt, batch, seq_len, tq=tq)
        x2d = _mm_residual(o, w_o, j, x2d, gate, final_norm_g[None], seq_len, tm=512,
                           final_norm=(layer == depth - 1))

    return x2d.reshape(batch, seq_len, d)
```

```python
import functools
import math

import jax
import jax.numpy as jnp
import numpy as np
from jax import lax
from jax.experimental import pallas as pl
from jax.experimental.pallas import tpu as pltpu

F32 = jnp.float32
BF16 = jnp.bfloat16

SSM_GROUP = 16
SSM_STATE = 64
N_HEADS = 16
N_KV = 4
HPG = N_HEADS // N_KV
HEAD_DIM = 128
N_BRANCH = 3
CMP_LEN = 32
CMP_STRIDE = 16
SEL_LEN = 64
SEL_TOPK = 16
WINDOW = 512
SEL_BONUS = 1e3
NEG = -1e30
EPS = 1e-6

LANES = 128
SUBLANES = 8
MXU_DIM = 256
VMEM_LIMIT = 56 * 1024 * 1024

PROJ_ROWS = 1024

GROUPS_PER_BLOCK = MXU_DIM // SSM_GROUP
STATES_PER_BLOCK = GROUPS_PER_BLOCK * SSM_STATE


def _cparams(sem):
    return pltpu.CompilerParams(dimension_semantics=sem, vmem_limit_bytes=VMEM_LIMIT)


def _gelu_tanh(x):
    return x * (0.5 * (1.0 + jnp.tanh(math.sqrt(2.0 / math.pi) * (x + 0.044715 * (x * x * x)))))


def _sigmoid(x):
    return 0.5 * jnp.tanh(0.5 * x) + 0.5


def _dot_nt(a, b):
    return lax.dot_general(a, b, (((1,), (1,)), ((), ())), preferred_element_type=F32)


def _cond_kernel(c_ref, w_ref, b_ref, o_ref):
    c = c_ref[...]
    ca = (c * _sigmoid(c)).astype(BF16)
    acc = jnp.dot(ca, w_ref[0].astype(BF16), preferred_element_type=F32)
    o_ref[0] = acc + b_ref[0]


def _cond_matmul(c_pad, w, b, tn=1024):
    nl, d, n = w.shape
    r = c_pad.shape[0]
    return pl.pallas_call(
        _cond_kernel,
        grid=(nl, n // tn),
        in_specs=[
            pl.BlockSpec((r, d), lambda l, j: (0, 0)),
            pl.BlockSpec((1, d, tn), lambda l, j: (l, 0, j)),
            pl.BlockSpec((1, 1, tn), lambda l, j: (l, 0, j)),
        ],
        out_specs=pl.BlockSpec((1, r, tn), lambda l, j: (l, 0, j)),
        out_shape=jax.ShapeDtypeStruct((nl, r, n), F32),
        compiler_params=_cparams(("parallel", "parallel")),
        name="cond_matmul",
    )(c_pad, w, b)


NORM_SLAB = 16


def _norm_modulate(x_ref, g_ref, sh_ref, sc_ref, h_ref):
    gain = g_ref[...] * (1.0 + sc_ref[0])
    shift = sh_ref[0]

    def slab(r, carry):
        rs = pl.ds(pl.multiple_of(r * NORM_SLAB, NORM_SLAB), NORM_SLAB)
        x = x_ref[rs, :]
        y = x * lax.rsqrt(jnp.mean(x * x, axis=-1, keepdims=True) + EPS)
        h_ref[rs, :] = (y * gain + shift).astype(BF16)
        return carry

    lax.fori_loop(0, x_ref.shape[0] // NORM_SLAB, slab, 0, unroll=8)


def _normmod_mm_kernel(x_ref, g_ref, sh_ref, sc_ref, w_ref, cs_ref, o_ref, h_ref, *, gated):
    @pl.when(pl.program_id(1) == 0)
    def _():
        _norm_modulate(x_ref, g_ref, sh_ref, sc_ref, h_ref)

    acc = jnp.dot(h_ref[...], w_ref[...], preferred_element_type=F32)
    if gated:
        acc = acc * (cs_ref[0:1, :] + cs_ref[1:2, :] * _sigmoid(acc))
    else:
        acc = acc * cs_ref[0:1, :]
    o_ref[...] = acc.astype(o_ref.dtype)


def _normmod_matmul(x2d, g, shift, scale, w, layer, colscale, seq_len, *, tm, tn, gated=False):
    m, d = x2d.shape
    n = w.shape[2]
    rows_per_batch = seq_len // tm
    return pl.pallas_call(
        functools.partial(_normmod_mm_kernel, gated=gated),
        grid=(m // tm, n // tn),
        in_specs=[
            pl.BlockSpec((tm, d), lambda i, j: (i, 0)),
            pl.BlockSpec((1, d), lambda i, j: (0, 0)),
            pl.BlockSpec((1, 1, d), lambda i, j: (i // rows_per_batch, 0, 0)),
            pl.BlockSpec((1, 1, d), lambda i, j: (i // rows_per_batch, 0, 0)),
            pl.BlockSpec((None, d, tn), lambda i, j: (layer, 0, j)),
            pl.BlockSpec((colscale.shape[0], tn), lambda i, j: (0, j)),
        ],
        out_specs=pl.BlockSpec((tm, tn), lambda i, j: (i, j)),
        out_shape=jax.ShapeDtypeStruct((m, n), BF16),
        scratch_shapes=[pltpu.VMEM((tm, d), BF16)],
        compiler_params=_cparams(("parallel", "arbitrary")),
        name="normmod_matmul",
    )(x2d, g, shift, scale, w, colscale)


KV_KCMP, KV_VCMP, KV_KSEL, KV_KWIN = range(4)
VT_SEL, VT_WIN = range(2)
_SEL_V_COL, _WIN_V_COL = 3, 5


def _kv_proj_kernel(x_ref, g_ref, sh_ref, sc_ref, w_ref, ks_ref, vt_ref, h_ref):
    j = pl.program_id(1)

    @pl.when(j == 0)
    def _():
        _norm_modulate(x_ref, g_ref, sh_ref, sc_ref, h_ref)

    acc = jnp.dot(h_ref[...], w_ref[...], preferred_element_type=F32)
    tk = vt_ref.shape[-1]
    is_value = (j == _SEL_V_COL) | (j == _WIN_V_COL)

    @pl.when(jnp.logical_not(is_value))
    def _():
        for s in range(N_KV):
            ks_ref[s] = acc[:, s * HEAD_DIM:(s + 1) * HEAD_DIM].astype(ks_ref.dtype)

    @pl.when(is_value)
    def _():
        for s in range(N_KV):
            for r in range(acc.shape[0] // tk):
                blk = acc[r * tk:(r + 1) * tk, s * HEAD_DIM:(s + 1) * HEAD_DIM]
                vt_ref[s, 0, r] = blk.T.astype(vt_ref.dtype)


def _kv_projection(x2d, g, shift, scale, w, batch, seq_len, *, tm, tk):
    m, d = x2d.shape
    tn = N_KV * HEAD_DIM
    assert w.shape[1] == 6 * tn
    rpb = seq_len // tm

    def slab_kind(j):
        return j - (j >= _SEL_V_COL).astype(jnp.int32) - (j >= _WIN_V_COL).astype(jnp.int32)

    return pl.pallas_call(
        _kv_proj_kernel,
        grid=(m // tm, 6),
        in_specs=[
            pl.BlockSpec((tm, d), lambda i, j: (i, 0)),
            pl.BlockSpec((1, d), lambda i, j: (0, 0)),
            pl.BlockSpec((1, 1, d), lambda i, j: (i // rpb, 0, 0)),
            pl.BlockSpec((1, 1, d), lambda i, j: (i // rpb, 0, 0)),
            pl.BlockSpec((d, tn), lambda i, j: (0, j)),
        ],
        out_specs=[
            pl.BlockSpec((N_KV, tm, HEAD_DIM), lambda i, j: (slab_kind(j), i, 0)),
            pl.BlockSpec((N_KV, 1, tm // tk, HEAD_DIM, tk),
                         lambda i, j: ((j > _SEL_V_COL).astype(jnp.int32), i // rpb, i % rpb, 0, 0)),
        ],
        out_shape=[
            jax.ShapeDtypeStruct((4 * N_KV, m, HEAD_DIM), BF16),
            jax.ShapeDtypeStruct((2 * N_KV, batch, seq_len // tk, HEAD_DIM, tk), BF16),
        ],
        scratch_shapes=[pltpu.VMEM((tm, d), BF16)],
        compiler_params=_cparams(("parallel", "arbitrary")),
        name="kv_projection",
    )(x2d, g, shift, scale, w)


def _mm_res_kernel(a_ref, w_ref, x_ref, gate_ref, fg_ref, o_ref, *, final_norm):
    acc = jnp.dot(a_ref[...], w_ref[...], preferred_element_type=F32)
    xn = x_ref[...] + gate_ref[0] * acc
    if final_norm:
        xn = xn * lax.rsqrt(jnp.mean(xn * xn, axis=-1, keepdims=True) + EPS) * fg_ref[...]
    o_ref[...] = xn


def _mm_residual(a, w, layer, x2d, gate, final_g, seq_len, *, tm, final_norm):
    m, k = a.shape
    n = w.shape[2]
    rows_per_batch = seq_len // tm
    return pl.pallas_call(
        functools.partial(_mm_res_kernel, final_norm=final_norm),
        grid=(m // tm,),
        in_specs=[
            pl.BlockSpec((tm, k), lambda i: (i, 0)),
            pl.BlockSpec((None, k, n), lambda i: (layer, 0, 0), pipeline_mode=pl.Buffered(1)),
            pl.BlockSpec((tm, n), lambda i: (i, 0)),
            pl.BlockSpec((1, 1, n), lambda i: (i // rows_per_batch, 0, 0)),
            pl.BlockSpec((1, n), lambda i: (0, 0)),
        ],
        out_specs=pl.BlockSpec((tm, n), lambda i: (i, 0)),
        out_shape=jax.ShapeDtypeStruct((m, n), F32),
        compiler_params=_cparams(("parallel",)),
        name="matmul_residual_final" if final_norm else "matmul_residual",
    )(a, w, x2d, gate, final_g)


def _glu_out_kernel(y_ref, z_ref, wg_ref, bg_ref, wo_ref, x_ref, gate_ref, fg_ref, o_ref, *, final_norm):
    y = y_ref[...]
    lin = jnp.dot(y, wg_ref[...], preferred_element_type=F32) + bg_ref[...]
    z = z_ref[...].astype(F32)
    t = ((y.astype(F32) * _sigmoid(lin)) * (z * _sigmoid(z))).astype(BF16)
    xn = x_ref[...] + gate_ref[0] * jnp.dot(t, wo_ref[...], preferred_element_type=F32)
    if final_norm:
        xn = xn * lax.rsqrt(jnp.mean(xn * xn, axis=-1, keepdims=True) + EPS) * fg_ref[...]
    o_ref[...] = xn


def _glu_out_residual(y, uz, w_glu, b_glu, w_out, layer, x2d, gate, final_g, seq_len, *, tm, final_norm):
    m, e = y.shape
    n = w_out.shape[2]
    rows_per_batch = seq_len // tm
    resident = pl.Buffered(1)
    return pl.pallas_call(
        functools.partial(_glu_out_kernel, final_norm=final_norm),
        grid=(m // tm,),
        in_specs=[
            pl.BlockSpec((tm, e), lambda i: (i, 0)),
            pl.BlockSpec((tm, e), lambda i: (i, 1)),
            pl.BlockSpec((None, e, e), lambda i: (layer, 0, 0), pipeline_mode=resident),
            pl.BlockSpec((1, e), lambda i: (0, 0)),
            pl.BlockSpec((None, e, n), lambda i: (layer, 0, 0), pipeline_mode=resident),
            pl.BlockSpec((tm, n), lambda i: (i, 0)),
            pl.BlockSpec((1, 1, n), lambda i: (i // rows_per_batch, 0, 0)),
            pl.BlockSpec((1, n), lambda i: (0, 0)),
        ],
        out_specs=pl.BlockSpec((tm, n), lambda i: (i, 0)),
        out_shape=jax.ShapeDtypeStruct((m, n), F32),
        compiler_params=_cparams(("parallel",)),
        name="s5_glu_out",
    )(y, uz, w_glu, b_glu, w_out, x2d, gate, final_g)


SCAN_ROW_PITCH = 12


def _s5_scan_kernel(u_ref, wb_ref, wc_ref, are_ref, aim_ref, d_ref, y_ref, s_ref, st_ref, *, steps, nblk):
    nslab = STATES_PER_BLOCK // LANES

    @pl.when(pl.program_id(1) == 0)
    def _():
        st_ref[...] = jnp.zeros_like(st_ref)

    for blk in range(nblk):
        bu = jnp.dot(u_ref[:, blk * MXU_DIM:(blk + 1) * MXU_DIM], wb_ref[blk], preferred_element_type=F32)
        for k in range(2 * nslab):
            s_ref[k, pl.ds(blk, steps, stride=SCAN_ROW_PITCH), :] = bu[:, k * LANES:(k + 1) * LANES]

    a_re = [are_ref[:, k * LANES:(k + 1) * LANES] for k in range(nslab)]
    a_im = [aim_ref[:, k * LANES:(k + 1) * LANES] for k in range(nslab)]

    def step(t, carry):
        xr, xi = carry
        r0 = t * SCAN_ROW_PITCH
        nr, ni = [], []
        for k in range(nslab):
            b_re = s_ref[k, pl.ds(r0, nblk), :]
            b_im = s_ref[nslab + k, pl.ds(r0, nblk), :]
            v_re = a_re[k] * xr[k] - a_im[k] * xi[k] + b_re
            v_im = a_re[k] * xi[k] + a_im[k] * xr[k] + b_im
            s_ref[k, pl.ds(r0, nblk), :] = v_re
            s_ref[nslab + k, pl.ds(r0, nblk), :] = v_im
            nr.append(v_re)
            ni.append(v_im)
        return tuple(nr), tuple(ni)

    init = (tuple(st_ref[k] for k in range(nslab)), tuple(st_ref[nslab + k] for k in range(nslab)))
    xr, xi = lax.fori_loop(0, steps, step, init, unroll=8)
    for k in range(nslab):
        st_ref[k] = xr[k]
        st_ref[nslab + k] = xi[k]

    for blk in range(nblk):
        xs = jnp.concatenate(
            [s_ref[k, pl.ds(blk, steps, stride=SCAN_ROW_PITCH), :].astype(BF16) for k in range(2 * nslab)], axis=1)
        cols = slice(blk * MXU_DIM, (blk + 1) * MXU_DIM)
        yb = jnp.dot(xs, wc_ref[blk], preferred_element_type=F32)
        yb = yb + d_ref[:, cols] * u_ref[:, cols].astype(F32)
        y_ref[:, cols] = _gelu_tanh(yb).astype(y_ref.dtype)


def _s5_scan(uz, wb, wc, a_re, a_im, d_skip, batch, seq_len, *, steps):
    m = uz.shape[0]
    nblk, _, two_states = wb.shape
    e = nblk * MXU_DIM
    chunks = seq_len // steps
    nslab2 = two_states // LANES
    return pl.pallas_call(
        functools.partial(_s5_scan_kernel, steps=steps, nblk=nblk),
        grid=(batch, chunks),
        in_specs=[
            pl.BlockSpec((steps, e), lambda b, c: (b * chunks + c, 0)),
            pl.BlockSpec(wb.shape, lambda b, c: (0, 0, 0), pipeline_mode=pl.Buffered(1)),
            pl.BlockSpec(wc.shape, lambda b, c: (0, 0, 0), pipeline_mode=pl.Buffered(1)),
            pl.BlockSpec(a_re.shape, lambda b, c: (0, 0)),
            pl.BlockSpec(a_im.shape, lambda b, c: (0, 0)),
            pl.BlockSpec((1, e), lambda b, c: (0, 0)),
        ],
        out_specs=pl.BlockSpec((steps, e), lambda b, c: (b * chunks + c, 0)),
        out_shape=jax.ShapeDtypeStruct((m, e), BF16),
        scratch_shapes=[
            pltpu.VMEM((nslab2, steps * SCAN_ROW_PITCH, LANES), F32),
            pltpu.VMEM((nslab2, nblk, LANES), F32),
        ],
        compiler_params=_cparams(("parallel", "arbitrary")),
        name="s5_scan",
    )(uz, wb, wc, a_re, a_im, d_skip)


def _s5_params(lam_re, lam_im, log_step, b_re, b_im, c_re, c_im):
    g, n = lam_re.shape
    dt = jnp.exp(log_step.astype(F32))[:, None]
    lr, li = lam_re.astype(F32), lam_im.astype(F32)
    mag = jnp.exp(lr * dt)
    a_re, a_im = mag * jnp.cos(li * dt), mag * jnp.sin(li * dt)
    den = lr * lr + li * li
    coef_re = ((a_re - 1.0) * lr + a_im * li) / den
    coef_im = (a_im * lr - (a_re - 1.0) * li) / den
    br, bi = b_re.astype(F32), b_im.astype(F32)
    bb_re = coef_re[..., None] * br - coef_im[..., None] * bi
    bb_im = coef_re[..., None] * bi + coef_im[..., None] * br
    nblk = g // GROUPS_PER_BLOCK

    def blockdiag(t, inner):
        rows = t.shape[1]
        cols = GROUPS_PER_BLOCK * inner
        expand = (jnp.arange(cols)[None, :] % inner == jnp.arange(inner)[:, None]).astype(F32)
        tiled = jnp.einsum('brn,nk->brk', t, expand, precision=lax.Precision.HIGHEST)
        same_group = (jnp.arange(rows)[:, None] // (rows // GROUPS_PER_BLOCK)
                      == jnp.arange(cols)[None, :] // inner)
        return jnp.where(same_group, tiled, 0.0)

    def blockdiag_in(bb):
        t = jnp.swapaxes(bb.reshape(nblk, GROUPS_PER_BLOCK, n, SSM_GROUP), 2, 3)
        return blockdiag(t.reshape(nblk, GROUPS_PER_BLOCK * SSM_GROUP, n), n)

    def blockdiag_out(cc):
        t = jnp.swapaxes(cc.astype(F32).reshape(nblk, GROUPS_PER_BLOCK, SSM_GROUP, n), 2, 3)
        return blockdiag(t.reshape(nblk, GROUPS_PER_BLOCK * n, SSM_GROUP), SSM_GROUP)

    wb = jnp.concatenate([blockdiag_in(bb_re), blockdiag_in(bb_im)], axis=2).astype(BF16)
    wc = jnp.concatenate([blockdiag_out(c_re), -blockdiag_out(c_im)], axis=1).astype(BF16)
    return wb, wc, a_re.reshape(nblk, GROUPS_PER_BLOCK * n), a_im.reshape(nblk, GROUPS_PER_BLOCK * n)


def _compress_kernel(x_ref, w1_ref, pe_ref, b1_ref, w2_ref, b2_ref, o_ref, sh_ref, *, n_cmp):
    nb = x_ref.shape[2]
    w1 = w1_ref[0]
    r = jnp.dot(x_ref[0, 0], w1, preferred_element_type=F32)
    pe = pe_ref[0]
    pe_hi = pe.astype(BF16)
    pe_lo = (pe - pe_hi.astype(F32)).astype(BF16)
    pw = (jnp.dot(pe_hi, w1, preferred_element_type=F32) + jnp.dot(pe_lo, w1, preferred_element_type=F32))
    const = pw[0:1, :HEAD_DIM] + pw[1:2, HEAD_DIM:] + b1_ref[0]
    sh_ref[pl.ds(0, nb), :] = r[:, HEAD_DIM:]
    sh_ref[pl.ds(nb, SUBLANES), :] = jnp.zeros((SUBLANES, HEAD_DIM), F32)
    hid = r[:, :HEAD_DIM] + sh_ref[pl.ds(1, nb), :] + const
    out = jnp.dot(_gelu_tanh(hid).astype(BF16), w2_ref[0], preferred_element_type=F32) + b2_ref[0]
    row = lax.broadcasted_iota(jnp.int32, out.shape, 0)
    o_ref[0, 0, 0] = jnp.where(row < n_cmp, out, 0.0).astype(o_ref.dtype)


def _compress(kv16, w1cat, pe2, b1, w2, b2, batch, n_cmp):
    nb, wide = kv16.shape[2], kv16.shape[3]
    return pl.pallas_call(
        functools.partial(_compress_kernel, n_cmp=n_cmp),
        grid=(2, N_KV, batch),
        in_specs=[
            pl.BlockSpec((1, 1, nb, wide), lambda s, g, b: (s * N_KV + g, b, 0, 0)),
            pl.BlockSpec((1, wide, 2 * HEAD_DIM), lambda s, g, b: (s, 0, 0)),
            pl.BlockSpec((1, 2 * SUBLANES, wide), lambda s, g, b: (s, 0, 0)),
            pl.BlockSpec((1, 1, HEAD_DIM), lambda s, g, b: (s, 0, 0)),
            pl.BlockSpec((1, HEAD_DIM, HEAD_DIM), lambda s, g, b: (s, 0, 0)),
            pl.BlockSpec((1, 1, HEAD_DIM), lambda s, g, b: (s, 0, 0)),
        ],
        out_specs=pl.BlockSpec((1, 1, 1, nb, HEAD_DIM), lambda s, g, b: (s, b, g, 0, 0)),
        out_shape=jax.ShapeDtypeStruct((2, batch, N_KV, nb, HEAD_DIM), BF16),
        scratch_shapes=[pltpu.VMEM((nb + SUBLANES, HEAD_DIM), F32)],
        compiler_params=_cparams(("parallel", "parallel", "parallel")),
        name="nsa_compress",
    )(kv16, w1cat, pe2, b1, w2, b2)


FLAG_COL = HEAD_DIM + HEAD_DIM // 2
V_ROWS = HEAD_DIM + 16
KEY_TILE = 512
MASK_BLOCK = 128


def _blocked_probs(s, tq, kind_of, m_prev=None):
    nkb, nlt, nj = s.shape[0] // MASK_BLOCK, s.shape[1] // MASK_BLOCK, tq // MASK_BLOCK
    krow = lax.broadcasted_iota(jnp.int32, (MASK_BLOCK, MASK_BLOCK), 0)
    tcol = lax.broadcasted_iota(jnp.int32, (MASK_BLOCK, MASK_BLOCK), 1)
    visible = {"le": krow <= tcol, "gt": krow > tcol}
    zero = jnp.zeros((MASK_BLOCK, MASK_BLOCK), BF16)
    ms, cols = [], []
    for c in range(nlt):
        ls = slice(c * MASK_BLOCK, (c + 1) * MASK_BLOCK)
        blks = {}
        for kk in range(nkb):
            kind = kind_of(kk, c % nj)
            if kind is not None:
                b = s[kk * MASK_BLOCK:(kk + 1) * MASK_BLOCK, ls]
                blks[kk] = b if kind == "full" else jnp.where(visible[kind], b, NEG)
        m_c = jnp.max(functools.reduce(jnp.maximum, blks.values()), axis=0, keepdims=True)
        if m_prev is not None:
            m_c = jnp.maximum(m_c, m_prev[:, ls])
        ms.append(m_c)
        cols.append(jnp.concatenate(
            [jnp.exp((blks[kk] - m_c).astype(BF16)) if kk in blks else zero for kk in range(nkb)], axis=0))
    return jnp.concatenate(ms, axis=1), jnp.concatenate(cols, axis=1)


def _nsa_t_kernel(q_ref, z0_ref, z1_ref, z2_ref, gt_ref, kc_ref, vct_ref, ks_ref, vst_ref, kw_ref, vwt_ref,
                  ovt_ref, o_ref, ka_ref, kwa_ref, qa_ref, qw_ref, m_ref, acc_ref, ot_ref, s_ref, gz_ref,
                  *, tq, n_cmp, n_slc, topk):
    qi = pl.program_id(2)
    t0 = qi * tq
    rows = HPG * tq
    nsp = ovt_ref.shape[0]
    seq_len, aug = ka_ref.shape

    @pl.when(qi == 0)
    def _():
        key_blk = lax.broadcasted_iota(jnp.int32, (seq_len, aug - HEAD_DIM), 0) // SEL_LEN
        col = lax.broadcasted_iota(jnp.int32, (seq_len, aug - HEAD_DIM), 1)
        ka_ref[:, 0:HEAD_DIM] = ks_ref[0, 0]
        ka_ref[:, HEAD_DIM:] = jnp.where(key_blk == col, 1.0, 0.0).astype(BF16)
        pcol = lax.broadcasted_iota(jnp.int32, (WINDOW, aug), 1)
        kwa_ref[0:WINDOW, :] = jnp.where(pcol == FLAG_COL, 1.0, 0.0).astype(BF16)
        kwa_ref[WINDOW:, 0:HEAD_DIM] = kw_ref[0, 0]
        kwa_ref[WINDOW:, HEAD_DIM:] = jnp.zeros((seq_len, aug - HEAD_DIM), BF16)

    ones_rows = jnp.where(lax.broadcasted_iota(jnp.int32, (V_ROWS - HEAD_DIM, vst_ref.shape[4]), 0) == 0,
                          1.0, 0.0).astype(BF16)

    def v_tile(vt_ref, idx):
        return jnp.concatenate([vt_ref[0, 0, idx], ones_rows], axis=0)

    q = q_ref[...]
    qh = jnp.concatenate([q[:, h * HEAD_DIM:(h + 1) * HEAD_DIM] for h in range(HPG)], axis=0)
    flag = jnp.where(lax.broadcasted_iota(jnp.int32, (tq, aug - FLAG_COL), 1) == 0, NEG, 0.0).astype(BF16)
    qa_ref[:, 0:HEAD_DIM] = qh
    qw_ref[:, 0:HEAD_DIM] = qh
    qw_ref[:, HEAD_DIM:FLAG_COL] = jnp.zeros((rows, FLAG_COL - HEAD_DIM), BF16)
    for h in range(HPG):
        qa_ref[h * tq:(h + 1) * tq, FLAG_COL:] = flag
        qw_ref[h * tq:(h + 1) * tq, FLAG_COL:] = flag

    tk = vst_ref.shape[4]
    n_win = WINDOW // MASK_BLOCK

    def window_kind(kk, j):
        d = kk - j
        return "gt" if d == 0 else "le" if d == n_win else "full" if 0 < d < n_win else None

    sw = _dot_nt(kwa_ref[pl.ds(pl.multiple_of(t0, tq), WINDOW + tq), :], qw_ref[...])
    _, pw = _blocked_probs(sw, tq, window_kind)
    o_w = jnp.zeros((V_ROWS, rows), F32)
    for dd in range((WINDOW + tq) // tk):
        vidx = jnp.maximum(qi * (tq // tk) + dd - WINDOW // tk, 0)
        o_w = o_w + jnp.dot(v_tile(vwt_ref, vidx), pw[dd * tk:(dd + 1) * tk], preferred_element_type=F32)
    ot_ref[2] = o_w[:HEAD_DIM] * (1.0 / o_w[HEAD_DIM:HEAD_DIM + 1])

    gates = _sigmoid(gt_ref[...].astype(F32))
    for br, z_ref in enumerate((z0_ref, z1_ref, z2_ref)):
        for h in range(HPG):
            cs = slice(h * HEAD_DIM, (h + 1) * HEAD_DIM)
            c = br * HPG + h
            gz_ref[br, :, cs] = gates[:, c:c + 1] * z_ref[:, cs].astype(F32)

    ncp = kc_ref.shape[3]
    st = _dot_nt(kc_ref[0, 0, 0], qh)
    n_idx = lax.broadcasted_iota(jnp.int32, (ncp, rows), 0)
    tpos = t0 + lax.broadcasted_iota(jnp.int32, (ncp, rows), 1) % tq
    cmask = (n_idx * CMP_STRIDE + (CMP_LEN - 1) <= tpos) & (n_idx < n_cmp)
    st = jnp.where(cmask, st, NEG)
    e = jnp.where(cmask, jnp.exp(st - jnp.max(st, axis=0, keepdims=True)), 0.0)
    l = jnp.sum(e, axis=0, keepdims=True)
    p = e * (1.0 / jnp.where(l > 0.0, l, 1.0))
    ot_ref[0] = jnp.dot(vct_ref[0, 0], p.astype(BF16), preferred_element_type=F32)
    p_cmp = p[:, 0:tq]
    for h in range(1, HPG):
        p_cmp = p_cmp + p[:, h * tq:(h + 1) * tq]

    p_hi = p_cmp.astype(BF16)
    r1 = p_cmp - p_hi.astype(F32)
    p_mid = r1.astype(BF16)
    p_lo = (r1 - p_mid.astype(F32)).astype(BF16)
    ovt = ovt_ref[...]
    p_slc = (jnp.dot(ovt, p_hi, preferred_element_type=F32) + jnp.dot(ovt, p_mid, preferred_element_type=F32)
             + jnp.dot(ovt, p_lo, preferred_element_type=F32))
    blk = lax.broadcasted_iota(jnp.int32, (nsp, tq), 0)
    cur = (t0 + lax.broadcasted_iota(jnp.int32, (nsp, tq), 1)) // SEL_LEN
    valid = (blk <= cur) & (blk < n_slc)
    forced = (blk == 0) | (blk == cur) | (blk == cur - 1)
    score = jnp.where(valid, p_slc + jnp.where(forced, SEL_BONUS, 0.0), -SEL_BONUS)
    nv = nsp // SUBLANES
    sc = [score[k * SUBLANES:(k + 1) * SUBLANES] for k in range(nv)]
    sub = lax.broadcasted_iota(jnp.int32, (SUBLANES, tq), 0)
    rank = [jnp.zeros((SUBLANES, tq), jnp.int32) for _ in range(nv)]
    for i in range(n_slc):
        si = jnp.broadcast_to(score[i:i + 1, :], (SUBLANES, tq))
        for k in range(nv):
            if k * SUBLANES > i:
                beats = si >= sc[k]
            elif k * SUBLANES + SUBLANES - 1 <= i:
                beats = si > sc[k]
            else:
                beats = (si > sc[k]) | ((si == sc[k]) & (sub > i - k * SUBLANES))
            rank[k] = rank[k] + jnp.where(beats, 1, 0)
    rank = jnp.concatenate(rank, axis=0)
    sel = (rank < topk) & (score > -0.5 * SEL_BONUS)
    selb = jnp.where(sel, 0.0, NEG).T.astype(BF16)
    for h in range(HPG):
        qa_ref[h * tq:(h + 1) * tq, HEAD_DIM:FLAG_COL] = selb


    def scores(k_tile):
        return _dot_nt(k_tile, qa_ref[...])

    def online_update(s, vt_tile):
        m_old = m_ref[...]
        m_new = jnp.maximum(m_old, jnp.max(s, axis=0, keepdims=True))
        alpha = jnp.exp(m_old - m_new)
        pt = jnp.exp((s - m_new).astype(BF16))
        acc_ref[...] = alpha * acc_ref[...] + jnp.dot(vt_tile, pt, preferred_element_type=F32)
        m_ref[...] = m_new

    m_ref[...] = jnp.full(m_ref.shape, NEG, F32)
    acc_ref[...] = jnp.zeros(acc_ref.shape, F32)
    n_full = t0 // tk
    s_ref[...] = scores(ka_ref[pl.ds(0, tk), :])

    def sel_body(kt, carry):
        k1 = pl.multiple_of((kt + 1) * tk, tk)
        s_next = scores(ka_ref[pl.ds(k1, tk), :])
        online_update(s_ref[...], v_tile(vst_ref, kt))
        s_ref[...] = s_next
        return carry

    lax.fori_loop(0, n_full, sel_body, 0)

    def diagonal_kind(kk, j):
        return "full" if kk < j else "le" if kk == j else None

    m_old = m_ref[...]
    m_new, pt = _blocked_probs(s_ref[...], tq, diagonal_kind, m_old)
    acc = jnp.exp(m_old - m_new) * acc_ref[...] + jnp.dot(v_tile(vst_ref, n_full), pt, preferred_element_type=F32)
    ot_ref[1] = acc[:HEAD_DIM] * (1.0 / acc[HEAD_DIM:HEAD_DIM + 1])

    for h in range(HPG):
        cs = slice(h * HEAD_DIM, (h + 1) * HEAD_DIM)
        tot = None
        for br in range(N_BRANCH):
            term = gz_ref[br, :, cs] * ot_ref[br, :, h * tq:(h + 1) * tq].T
            tot = term if tot is None else tot + term
        o_ref[:, cs] = tot.astype(o_ref.dtype)


def _nsa_attention_t(proj, kcv, vct, kslab, vt, ovt, batch, seq_len, *, tq):
    m = proj.shape[0]
    gw = HPG * HEAD_DIM
    qt = seq_len // tq
    n_cmp = (seq_len - CMP_LEN) // CMP_STRIDE + 1
    n_slc = seq_len // SEL_LEN
    z_base = N_KV
    gate_base = (N_KV + N_BRANCH * N_KV) * (gw // LANES)
    ncp = kcv.shape[3]
    aug = 2 * HEAD_DIM
    rows = HPG * tq
    tk = vt.shape[-1]

    def zspec(br):
        return pl.BlockSpec((tq, gw), lambda b, g, i: (b * qt + i, z_base + br * N_KV + g))

    def per_group(arr, kind):
        return pl.BlockSpec((1, 1) + arr.shape[2:], lambda b, g, i: (kind * N_KV + g, b) + (0,) * (arr.ndim - 2))

    return pl.pallas_call(
        functools.partial(_nsa_t_kernel, tq=tq, n_cmp=n_cmp, n_slc=n_slc, topk=min(SEL_TOPK, n_slc)),
        grid=(batch, N_KV, qt),
        in_specs=[
            pl.BlockSpec((tq, gw), lambda b, g, i: (b * qt + i, g)),
            zspec(0), zspec(1), zspec(2),
            pl.BlockSpec((tq, LANES), lambda b, g, i: (b * qt + i, gate_base + g)),
            pl.BlockSpec((1, 1, 1, ncp, HEAD_DIM), lambda b, g, i: (0, b, g, 0, 0)),
            pl.BlockSpec((1, 1, HEAD_DIM, ncp), lambda b, g, i: (b, g, 0, 0)),
            per_group(kslab, KV_KSEL), per_group(vt, VT_SEL), per_group(kslab, KV_KWIN), per_group(vt, VT_WIN),
            pl.BlockSpec(ovt.shape, lambda b, g, i: (0, 0)),
        ],
        out_specs=pl.BlockSpec((tq, gw), lambda b, g, i: (b * qt + i, g)),
        out_shape=jax.ShapeDtypeStruct((m, N_KV * gw), BF16),
        scratch_shapes=[
            pltpu.VMEM((seq_len, aug), BF16),
            pltpu.VMEM((seq_len + WINDOW, aug), BF16),
            pltpu.VMEM((rows, aug), BF16),
            pltpu.VMEM((rows, aug), BF16),
            pltpu.VMEM((1, rows), F32),
            pltpu.VMEM((V_ROWS, rows), F32),
            pltpu.VMEM((N_BRANCH, HEAD_DIM, rows), F32),
            pltpu.VMEM((tk, rows), F32),
            pltpu.VMEM((N_BRANCH, tq, gw), F32),
        ],
        compiler_params=_cparams(("parallel", "parallel", "arbitrary")),
        name="nsa_attention",
    )(proj, proj, proj, proj, proj, kcv, vct, kslab, vt, kslab, vt, ovt)


def _selection_overlap_t(n_cmp, ncp, n_slc, nsp):
    c0 = np.arange(ncp)[None, :] * CMP_STRIDE
    s0 = np.arange(nsp)[:, None] * SEL_LEN
    ov = np.clip(np.minimum(c0 + CMP_LEN, s0 + SEL_LEN) - np.maximum(c0, s0), 0, None) / CMP_STRIDE
    ov = ov * (np.arange(ncp)[None, :] < n_cmp) * (np.arange(nsp)[:, None] < n_slc)
    return jnp.asarray(ov, dtype=BF16)


def _s5_layer(x2d, mods, norm_g, w_in, layer, s5p, d_skip, w_glu, b_glu, w_out, batch, seq_len, final_g,
              final_norm):
    d = x2d.shape[1]
    shift, scale, gate = (mods[:, None, i * d:(i + 1) * d] for i in range(3))
    e = w_glu.shape[1]
    uz = _normmod_matmul(x2d, norm_g[None], shift, scale, w_in, layer, jnp.ones((1, 2 * e), F32),
                         seq_len, tm=PROJ_ROWS, tn=1024)
    wb, wc, a_re, a_im = s5p
    y = _s5_scan(uz, wb, wc, a_re, a_im, d_skip[None].astype(F32), batch, seq_len, steps=256)
    return _glu_out_residual(y, uz, w_glu, b_glu[None].astype(F32), w_out, layer, x2d, gate,
                             final_g[None], seq_len, tm=512, final_norm=final_norm)


def _qg_weight(w_qg):
    nl, d, _ = w_qg.shape
    att = N_HEADS * HEAD_DIM
    g_end = att + N_BRANCH * N_HEADS
    w_qg = w_qg.astype(BF16)
    wg = w_qg[:, :, att:g_end].reshape(nl, d, N_BRANCH, N_KV, HPG)
    wg = jnp.transpose(wg, (0, 1, 3, 2, 4)).reshape(nl, d, N_KV, N_BRANCH * HPG)
    wg = jnp.pad(wg, ((0, 0), (0, 0), (0, 0), (0, LANES - N_BRANCH * HPG))).reshape(nl, d, N_KV * LANES)
    w = jnp.concatenate([w_qg[:, :, :att], w_qg[:, :, g_end:], wg], axis=2)
    n_z = N_BRANCH * att
    n_gate = w.shape[2] - att - n_z
    linear = jnp.concatenate([jnp.full((att,), HEAD_DIM ** -0.5, F32), jnp.zeros((n_z,), F32), jnp.ones((n_gate,), F32)])
    silu = jnp.concatenate([jnp.zeros((att,), F32), jnp.ones((n_z,), F32), jnp.zeros((n_gate,), F32)])
    return w, jnp.stack([linear, silu])


def kernel(x, c, norm_g, mod_w, mod_b, ssm_w_in, ssm_lam_re, ssm_lam_im, ssm_log_step, ssm_b_re, ssm_b_im, ssm_c_re, ssm_c_im, ssm_d, ssm_w_glu, ssm_b_glu, ssm_w_out, kv_norm_g, kv_mod_w, kv_mod_b, w_kv, cmp_pe, cmp_w1, cmp_b1, cmp_w2, cmp_b2, nsa_w_qg, nsa_w_o, final_norm_g):
    batch, seq_len, d = x.shape
    depth = mod_w.shape[0]
    n_a = ssm_w_in.shape[0]
    m = batch * seq_len
    x2d = x.reshape(m, d)

    c_pad = jnp.pad(c, ((0, 2 * SUBLANES - batch), (0, 0)))
    mods = _cond_matmul(c_pad, mod_w, mod_b[:, None])[:, :batch]
    kv_mods = _cond_matmul(c_pad, kv_mod_w[None], kv_mod_b[None, None])[0, :batch]

    w_in, w_glu, w_out = ssm_w_in.astype(BF16), ssm_w_glu.astype(BF16), ssm_w_out.astype(BF16)
    for layer in range(n_a):
        s5p = _s5_params(ssm_lam_re[layer], ssm_lam_im[layer], ssm_log_step[layer], ssm_b_re[layer],
                         ssm_b_im[layer], ssm_c_re[layer], ssm_c_im[layer])
        x2d = _s5_layer(x2d, mods[layer], norm_g[layer], w_in, layer, s5p, ssm_d[layer], w_glu,
                        ssm_b_glu[layer], w_out, batch, seq_len, final_norm_g,
                        final_norm=(layer == depth - 1))

    tq = KEY_TILE
    assert seq_len % KEY_TILE == 0 and WINDOW % KEY_TILE == 0 and WINDOW % MASK_BLOCK == 0
    kv_shift, kv_scale = kv_mods[:, None, :d], kv_mods[:, None, d:]
    kslab, vt = _kv_projection(x2d, kv_norm_g[None], kv_shift, kv_scale, w_kv.astype(BF16), batch, seq_len,
                               tm=PROJ_ROWS, tk=KEY_TILE)
    kslab = kslab.reshape(4 * N_KV, batch, seq_len, HEAD_DIM)

    n_cmp = (seq_len - CMP_LEN) // CMP_STRIDE + 1
    n_slc = seq_len // SEL_LEN
    nb16 = seq_len // CMP_STRIDE
    half = CMP_STRIDE * HEAD_DIM
    kv16 = kslab[:2 * N_KV].reshape(2 * N_KV, batch, nb16, half)
    w1cat = jnp.concatenate([cmp_w1[:, :half], cmp_w1[:, half:]], axis=2).astype(BF16)
    pe2 = jnp.pad(cmp_pe.reshape(2, 2, half), ((0, 0), (0, 2 * SUBLANES - 2), (0, 0)))
    kcv = _compress(kv16, w1cat, pe2, cmp_b1[:, None], cmp_w2.astype(BF16), cmp_b2[:, None], batch, n_cmp)

    nsp = FLAG_COL - HEAD_DIM
    assert n_slc <= nsp
    vct = jnp.swapaxes(kcv[1], -1, -2)
    ovt = _selection_overlap_t(n_cmp, nb16, n_slc, nsp)

    wq, cs = _qg_weight(nsa_w_qg)
    w_o = nsa_w_o.astype(BF16)
    for layer in range(n_a, depth):
        j = layer - n_a
        shift, scale, gate = (mods[layer][:, None, i * d:(i + 1) * d] for i in range(3))
        proj = _normmod_matmul(x2d, norm_g[layer][None], shift, scale, wq, j, cs, seq_len, tm=PROJ_ROWS, tn=512,
                               gated=True)
        o = _nsa_attention_t(proj, kcv, vct, kslab, vt, ovt, batch, seq_len, tq=tq)
        x2d = _mm_residual(o, w_o, j, x2d, gate, final_norm_g[None], seq_len, tm=512,
                           final_norm=(layer == depth - 1))

    return x2d.reshape(batch, seq_len, d)
```
